```python
import jax, jax.numpy as jnp
from jax import lax
import numpy as np

D_MODEL = 1024
BATCH = 8
SEQ = 2048
DEPTH = 4
DEC_BATCH = 128
DEC_SEQ = 1
PAST_LEN = 8192
PAGE_SIZE = 128

N_MIXERS = 4
N_LAYERS_A = (DEPTH + 3) // N_MIXERS
N_LAYERS_B = (DEPTH + 2) // N_MIXERS
N_LAYERS_C = (DEPTH + 1) // N_MIXERS
N_LAYERS_D = DEPTH // N_MIXERS

HEAD_DIM = 128
N_HEADS = D_MODEL // HEAD_DIM
ROPE_THETA = 500000.0
ROT_DIM = HEAD_DIM // 4
Q_BLOCK = 128
RMS_EPS = 1e-6
D_FF = 2816

MLA_HEADS = N_HEADS
MLA_NOPE = 128
MLA_ROPE = 64
MLA_V = 128
MLA_Q_LORA = 384
MLA_KV_LORA = 256
DIL_PATTERNS = ((128, 1), (512, 4), (2048, 16))
DIL_GROUPS = 3
DIL_HEADS = N_HEADS
FOX_HEADS = N_HEADS
FOX_KV_HEADS = 2
SB_HEADS = N_HEADS
SB_KV_HEADS = 2

kernel_name = "hybrid_mla_dilated_fox_stickbreak_macaron_step"


def rms_norm(x, g):
    xf = x.astype(jnp.float32)
    y = xf * lax.rsqrt(jnp.mean(xf * xf, axis=-1, keepdims=True) + RMS_EPS)
    return (y * g.astype(jnp.float32)).astype(x.dtype)


def apply_rope(x, pos, rot_dim):
    half = rot_dim // 2
    inv_freq = jnp.power(ROPE_THETA, -jnp.arange(half, dtype=jnp.float32) * 2.0 / rot_dim)
    ang = pos.astype(jnp.float32)[:, None] * inv_freq[None, :]
    cos = jnp.cos(ang)[None, :, None, :]
    sin = jnp.sin(ang)[None, :, None, :]
    xr = x[..., :rot_dim].astype(jnp.float32)
    x1, x2 = xr[..., :half], xr[..., half:]
    rot = jnp.concatenate([x1 * cos - x2 * sin, x2 * cos + x1 * sin], axis=-1)
    return jnp.concatenate([rot.astype(x.dtype), x[..., rot_dim:]], axis=-1)


def macaron_half(x, g, w_gu, w_down):
    gu = rms_norm(x, g) @ w_gu
    return x + 0.5 * ((jax.nn.silu(gu[..., :D_FF]) * gu[..., D_FF:]) @ w_down)


def gather_pages(pool, layer, page_table):
    g = pool[layer, page_table]
    return g.reshape(g.shape[0], g.shape[1] * g.shape[2], *g.shape[3:])


def sweep_q_blocks(fn, qs, q_pos):
    T = q_pos.shape[0]
    if T > Q_BLOCK and T % Q_BLOCK == 0:
        nb = T // Q_BLOCK
        qb = tuple(jnp.swapaxes(a.reshape(a.shape[0], nb, Q_BLOCK, *a.shape[2:]), 0, 1) for a in qs)
        pb = q_pos.reshape(nb, Q_BLOCK)
        out = lax.map(lambda args: fn(*args[0], args[1]), (qb, pb))
        out = jnp.swapaxes(out, 0, 1)
        return out.reshape(out.shape[0], T, *out.shape[3:])
    return fn(*qs, q_pos)


def mla_project(h, pos, w_down, q_norm_g, kv_norm_g, w_uq):
    B, T, _ = h.shape
    d = h @ w_down
    c_q = rms_norm(d[..., :MLA_Q_LORA], q_norm_g)
    c_kv = rms_norm(d[..., MLA_Q_LORA:MLA_Q_LORA + MLA_KV_LORA], kv_norm_g)
    k_rope = apply_rope(d[:, :, None, MLA_Q_LORA + MLA_KV_LORA:], pos, MLA_ROPE)[:, :, 0]
    q = (c_q @ w_uq).reshape(B, T, MLA_HEADS, MLA_NOPE + MLA_ROPE)
    q_nope = q[..., :MLA_NOPE]
    q_rope = apply_rope(q[..., MLA_NOPE:], pos, MLA_ROPE)
    latent = jnp.concatenate([c_kv, k_rope], axis=-1)
    return q_nope, q_rope, latent


def mla_attend(q_nope, q_rope, q_pos, latent, k_pos, w_uk, w_uv, w_o):
    B, T = q_nope.shape[:2]
    c_kv = latent[..., :MLA_KV_LORA]
    k_rope = latent[..., MLA_KV_LORA:]
    q_lat = jnp.einsum('bthn,chn->bthc', q_nope, w_uk)
    scale = (MLA_NOPE + MLA_ROPE) ** -0.5

    def block(ql, qr, qp):
        s = (jnp.einsum('bthc,bsc->bhts', ql, c_kv)
             + jnp.einsum('bthr,bsr->bhts', qr, k_rope)).astype(jnp.float32) * scale
        s = jnp.where(k_pos[None, :] <= qp[:, None], s, -jnp.inf)
        p = jax.nn.softmax(s, axis=-1).astype(c_kv.dtype)
        return jnp.einsum('bhts,bsc->bthc', p, c_kv)

    o_lat = sweep_q_blocks(block, (q_lat, q_rope), q_pos)
    o = jnp.einsum('bthc,chd->bthd', o_lat, w_uv).reshape(B, T, MLA_HEADS * MLA_V)
    return o @ w_o


def dilated_project(h, pos, w_in):
    B, T, _ = h.shape
    qkv = (h @ w_in).reshape(B, T, 3, DIL_GROUPS * DIL_HEADS, HEAD_DIM)
    q = apply_rope(qkv[:, :, 0], pos, ROT_DIM).reshape(B, T, DIL_GROUPS, DIL_HEADS, HEAD_DIM)
    k = apply_rope(qkv[:, :, 1], pos, ROT_DIM).reshape(B, T, DIL_GROUPS, DIL_HEADS, HEAD_DIM)
    v = qkv[:, :, 2].reshape(B, T, DIL_GROUPS, DIL_HEADS, HEAD_DIM)
    return q, jnp.stack([k, v], axis=2)


def dilated_group_prompt(q, kv, dil, n):
    B, S, H, D = q.shape
    L = S // dil
    nb = -(-L // n)
    Lp = nb * n

    def to_sub(a):
        a = a.reshape(B, L, dil, H, D).transpose(0, 2, 1, 3, 4)
        return jnp.pad(a, ((0, 0), (0, 0), (0, Lp - L), (0, 0), (0, 0)))

    def window(a):
        a = jnp.pad(a, ((0, 0), (0, 0), (n, 0), (0, 0), (0, 0))).reshape(B, dil, nb + 1, n, H, D)
        return jnp.concatenate([a[:, :, :-1], a[:, :, 1:]], axis=3)

    qb = to_sub(q).reshape(B, dil, nb, n, H, D)
    kw = window(to_sub(kv[:, :, 0]))
    vw = window(to_sub(kv[:, :, 1]))
    s = jnp.einsum('brnqhd,brnkhd->brnhqk', qb, kw).astype(jnp.float32) * (HEAD_DIM ** -0.5)
    qi = jnp.arange(n)[:, None]
    kj = jnp.arange(2 * n)[None, :]
    dist = qi + n - kj
    key_idx = jnp.arange(nb)[:, None, None] * n - n + kj[None]
    mask = (dist >= 0) & (dist <= n) & (key_idx >= 0)
    s = jnp.where(mask[None, None, :, None], s, -jnp.inf)
    lse = jax.nn.logsumexp(s, axis=-1)
    p = jnp.exp(s - lse[..., None]).astype(q.dtype)
    o = jnp.einsum('brnhqk,brnkhd->brnqhd', p, vw)
    o = o.reshape(B, dil, Lp, H, D)[:, :, :L].transpose(0, 2, 1, 3, 4).reshape(B, S, H, D)
    lse = lse.transpose(0, 1, 2, 4, 3).reshape(B, dil, Lp, H)[:, :, :L].transpose(0, 2, 1, 3).reshape(B, S, H)
    return o, lse


def dilated_group_sample(q, kv_all, dil, n):
    B, T = q.shape[:2]
    L_buf = kv_all.shape[1] - T
    idx = L_buf + jnp.arange(T)[:, None] - dil * jnp.arange(n + 1)[None, :]
    valid = idx >= 0
    g = kv_all[:, jnp.maximum(idx, 0)]
    s = jnp.einsum('bthd,btkhd->bthk', q, g[:, :, :, 0]).astype(jnp.float32) * (HEAD_DIM ** -0.5)
    s = jnp.where(valid[None, :, None, :], s, -jnp.inf)
    lse = jax.nn.logsumexp(s, axis=-1)
    p = jnp.exp(s - lse[..., None]).astype(q.dtype)
    o = jnp.einsum('bthk,btkhd->bthd', p, g[:, :, :, 1])
    return o, lse


def dilated_merge(outs, lses, w_o):
    o = jnp.stack(outs, axis=0)
    wts = jax.nn.softmax(jnp.stack(lses, axis=0), axis=0).astype(o.dtype)
    o = jnp.einsum('gbth,gbthd->bthd', wts, o)
    return o.reshape(o.shape[0], o.shape[1], DIL_HEADS * HEAD_DIM) @ w_o


def fox_project(h, w_in, b_f):
    B, T, _ = h.shape
    nq, nkv = FOX_HEADS * HEAD_DIM, FOX_KV_HEADS * HEAD_DIM
    p = h @ w_in
    q = p[..., :nq].reshape(B, T, FOX_HEADS, HEAD_DIM)
    kv = p[..., nq:nq + 2 * nkv].reshape(B, T, 2, FOX_KV_HEADS, HEAD_DIM)
    logf = jax.nn.log_sigmoid((p[..., nq + 2 * nkv:] + b_f).astype(jnp.float32))
    return q, kv, logf


def fox_attend(q, q_pos, kv_all, logf_all, k_pos, w_o):
    B, T = q.shape[:2]
    S = kv_all.shape[1]
    rep = FOX_HEADS // FOX_KV_HEADS
    c = jnp.cumsum(logf_all.astype(jnp.float32), axis=1)
    c_q = c[:, S - T:]
    c_k = c.reshape(B, S, FOX_KV_HEADS, rep).transpose(0, 2, 3, 1)
    k, v = kv_all[:, :, 0], kv_all[:, :, 1]
    qg = q.reshape(B, T, FOX_KV_HEADS, rep, HEAD_DIM)
    scale = HEAD_DIM ** -0.5

    def block(qb, cqb, qp):
        s = jnp.einsum('btgrd,bsgd->bgrts', qb, k).astype(jnp.float32) * scale
        cq = cqb.reshape(B, -1, FOX_KV_HEADS, rep).transpose(0, 2, 3, 1)[..., None]
        s = s + cq - c_k[:, :, :, None, :]
        s = jnp.where(k_pos[None, :] <= qp[:, None], s, -jnp.inf)
        p = jax.nn.softmax(s, axis=-1).astype(v.dtype)
        return jnp.einsum('bgrts,bsgd->btgrd', p, v)

    o = sweep_q_blocks(block, (qg, c_q), q_pos)
    return o.reshape(B, T, FOX_HEADS * HEAD_DIM) @ w_o


def sb_project(h, w_in):
    B, T, _ = h.shape
    nq, nkv = SB_HEADS * HEAD_DIM, SB_KV_HEADS * HEAD_DIM
    p = h @ w_in
    q = p[..., :nq].reshape(B, T, SB_HEADS, HEAD_DIM)
    kv = p[..., nq:nq + 2 * nkv].reshape(B, T, 2, SB_KV_HEADS, HEAD_DIM)
    return q, kv


def sb_attend(q, q_pos, kv_all, k_pos, w_o):
    B, T = q.shape[:2]
    rep = SB_HEADS // SB_KV_HEADS
    k, v = kv_all[:, :, 0], kv_all[:, :, 1]
    qg = q.reshape(B, T, SB_KV_HEADS, rep, HEAD_DIM)
    scale = HEAD_DIM ** -0.5

    def block(qb, qp):
        z = jnp.einsum('btgrd,bsgd->bgrts', qb, k).astype(jnp.float32) * scale
        mask = k_pos[None, :] < qp[:, None]
        log_keep = jnp.where(mask, jax.nn.log_sigmoid(-z), 0.0)
        later = lax.cumsum(log_keep, axis=z.ndim - 1, reverse=True) - log_keep
        a = jnp.where(mask, jnp.exp(jax.nn.log_sigmoid(z) + later), 0.0).astype(v.dtype)
        return jnp.einsum('bgrts,bsgd->btgrd', a, v)

    o = sweep_q_blocks(block, (qg,), q_pos)
    return o.reshape(B, T, SB_HEADS * HEAD_DIM) @ w_o


def setup_inputs(seed: int = 0) -> dict:
    key = jax.random.key(seed)
    ks = iter(jax.random.split(key, 40))
    f32 = jnp.float32

    def nrm(shape, scale=1.0):
        return scale * jax.random.normal(next(ks), shape, f32)

    n_pages = PAST_LEN // PAGE_SIZE
    n_used = DEC_BATCH * n_pages
    n_pool = n_used + max(1, n_used // 4)
    x_prompt = nrm((BATCH, SEQ, D_MODEL))
    x_sample = nrm((DEC_BATCH, DEC_SEQ, D_MODEL))
    cache_a_latent = nrm((N_LAYERS_A, n_pool, PAGE_SIZE, MLA_KV_LORA + MLA_ROPE))
    cache_b_kv_w128 = nrm((N_LAYERS_B, DEC_BATCH, min(DIL_PATTERNS[0][0], PAST_LEN), 2, DIL_HEADS, HEAD_DIM))
    cache_b_kv_w512 = nrm((N_LAYERS_B, DEC_BATCH, min(DIL_PATTERNS[1][0], PAST_LEN), 2, DIL_HEADS, HEAD_DIM))
    cache_b_kv_w2048 = nrm((N_LAYERS_B, DEC_BATCH, min(DIL_PATTERNS[2][0], PAST_LEN), 2, DIL_HEADS, HEAD_DIM))
    cache_c_kv = nrm((N_LAYERS_C, n_pool, PAGE_SIZE, 2, FOX_KV_HEADS, HEAD_DIM))
    cache_c_logf = jax.nn.log_sigmoid(nrm((N_LAYERS_C, n_pool, PAGE_SIZE, FOX_HEADS)))
    cache_d_kv = nrm((N_LAYERS_D, n_pool, PAGE_SIZE, 2, SB_KV_HEADS, HEAD_DIM))
    page_table = jax.random.permutation(next(ks), n_pool)[:n_used].reshape(DEC_BATCH, n_pages).astype(jnp.int32)

    ffn_norm_g = 1.0 + 0.01 * nrm((DEPTH, 2, D_MODEL))
    ffn_w_gu = nrm((DEPTH, 2, D_MODEL, 2 * D_FF), D_MODEL ** -0.5)
    ffn_w_down = nrm((DEPTH, 2, D_FF, D_MODEL), D_FF ** -0.5)
    mix_norm_g = 1.0 + 0.01 * nrm((DEPTH, D_MODEL))
    a_w_down = nrm((N_LAYERS_A, D_MODEL, MLA_Q_LORA + MLA_KV_LORA + MLA_ROPE), D_MODEL ** -0.5)
    a_q_norm_g = 1.0 + 0.01 * nrm((N_LAYERS_A, MLA_Q_LORA))
    a_kv_norm_g = 1.0 + 0.01 * nrm((N_LAYERS_A, MLA_KV_LORA))
    a_w_uq = nrm((N_LAYERS_A, MLA_Q_LORA, MLA_HEADS * (MLA_NOPE + MLA_ROPE)), MLA_Q_LORA ** -0.5)
    a_w_uk = nrm((N_LAYERS_A, MLA_KV_LORA, MLA_HEADS, MLA_NOPE), MLA_KV_LORA ** -0.5)
    a_w_uv = nrm((N_LAYERS_A, MLA_KV_LORA, MLA_HEADS, MLA_V), MLA_KV_LORA ** -0.5)
    a_w_o = nrm((N_LAYERS_A, MLA_HEADS * MLA_V, D_MODEL), (MLA_HEADS * MLA_V) ** -0.5)
    b_w_in = nrm((N_LAYERS_B, D_MODEL, 3 * DIL_GROUPS * DIL_HEADS * HEAD_DIM), D_MODEL ** -0.5)
    b_w_o = nrm((N_LAYERS_B, DIL_HEADS * HEAD_DIM, D_MODEL), (DIL_HEADS * HEAD_DIM) ** -0.5)
    c_w_in = nrm((N_LAYERS_C, D_MODEL, FOX_HEADS * HEAD_DIM + 2 * FOX_KV_HEADS * HEAD_DIM + FOX_HEADS), D_MODEL ** -0.5)
    c_b_f = 0.1 * nrm((N_LAYERS_C, FOX_HEADS))
    c_w_o = nrm((N_LAYERS_C, FOX_HEADS * HEAD_DIM, D_MODEL), (FOX_HEADS * HEAD_DIM) ** -0.5)
    d_w_in = nrm((N_LAYERS_D, D_MODEL, SB_HEADS * HEAD_DIM + 2 * SB_KV_HEADS * HEAD_DIM), D_MODEL ** -0.5)
    d_w_o = nrm((N_LAYERS_D, SB_HEADS * HEAD_DIM, D_MODEL), (SB_HEADS * HEAD_DIM) ** -0.5)
    final_norm_g = 1.0 + 0.01 * nrm((D_MODEL,))
    return {
        "x_prompt": x_prompt, "x_sample": x_sample,
        "cache_a_latent": cache_a_latent,
        "cache_b_kv_w128": cache_b_kv_w128, "cache_b_kv_w512": cache_b_kv_w512, "cache_b_kv_w2048": cache_b_kv_w2048,
        "cache_c_kv": cache_c_kv, "cache_c_logf": cache_c_logf, "cache_d_kv": cache_d_kv,
        "page_table": page_table,
        "ffn_norm_g": ffn_norm_g, "ffn_w_gu": ffn_w_gu, "ffn_w_down": ffn_w_down, "mix_norm_g": mix_norm_g,
        "a_w_down": a_w_down, "a_q_norm_g": a_q_norm_g, "a_kv_norm_g": a_kv_norm_g, "a_w_uq": a_w_uq,
        "a_w_uk": a_w_uk, "a_w_uv": a_w_uv, "a_w_o": a_w_o,
        "b_w_in": b_w_in, "b_w_o": b_w_o,
        "c_w_in": c_w_in, "c_b_f": c_b_f, "c_w_o": c_w_o,
        "d_w_in": d_w_in, "d_w_o": d_w_o,
        "final_norm_g": final_norm_g,
    }


def reference(x_prompt, x_sample, cache_a_latent, cache_b_kv_w128, cache_b_kv_w512, cache_b_kv_w2048,
              cache_c_kv, cache_c_logf, cache_d_kv, page_table,
              ffn_norm_g, ffn_w_gu, ffn_w_down, mix_norm_g,
              a_w_down, a_q_norm_g, a_kv_norm_g, a_w_uq, a_w_uk, a_w_uv, a_w_o,
              b_w_in, b_w_o, c_w_in, c_b_f, c_w_o, d_w_in, d_w_o, final_norm_g):
    S = x_prompt.shape[1]
    T = x_sample.shape[1]
    P = page_table.shape[1] * PAGE_SIZE
    pos_p = jnp.arange(S, dtype=jnp.int32)
    pos_s = P + jnp.arange(T, dtype=jnp.int32)
    kpos_s = jnp.arange(P + T, dtype=jnp.int32)
    b_caches = (cache_b_kv_w128, cache_b_kv_w512, cache_b_kv_w2048)

    a_p, a_s = [], []
    b_p = [[] for _ in range(DIL_GROUPS)]
    b_s = [[] for _ in range(DIL_GROUPS)]
    ckv_p, ckv_s, clf_p, clf_s = [], [], [], []
    dkv_p, dkv_s = [], []

    xp, xs = x_prompt, x_sample
    for i in range(DEPTH):
        m, j = i % N_MIXERS, i // N_MIXERS
        xp = macaron_half(xp, ffn_norm_g[i, 0], ffn_w_gu[i, 0], ffn_w_down[i, 0])
        xs = macaron_half(xs, ffn_norm_g[i, 0], ffn_w_gu[i, 0], ffn_w_down[i, 0])
        hp = rms_norm(xp, mix_norm_g[i])
        hs = rms_norm(xs, mix_norm_g[i])
        if m == 0:
            qn_p, qr_p, lat_p = mla_project(hp, pos_p, a_w_down[j], a_q_norm_g[j], a_kv_norm_g[j], a_w_uq[j])
            qn_s, qr_s, lat_s = mla_project(hs, pos_s, a_w_down[j], a_q_norm_g[j], a_kv_norm_g[j], a_w_uq[j])
            lat_all = jnp.concatenate([gather_pages(cache_a_latent, j, page_table), lat_s.astype(cache_a_latent.dtype)], axis=1)
            mp = mla_attend(qn_p, qr_p, pos_p, lat_p, pos_p, a_w_uk[j], a_w_uv[j], a_w_o[j])
            ms = mla_attend(qn_s, qr_s, pos_s, lat_all, kpos_s, a_w_uk[j], a_w_uv[j], a_w_o[j])
            a_p.append(lat_p)
            a_s.append(lat_s)
        elif m == 1:
            q_p, kv_p = dilated_project(hp, pos_p, b_w_in[j])
            q_s, kv_s = dilated_project(hs, pos_s, b_w_in[j])
            outs_p, lses_p, outs_s, lses_s = [], [], [], []
            for g, (win, dil) in enumerate(DIL_PATTERNS):
                kvg_p = kv_p[:, :, :, g]
                kvg_s = kv_s[:, :, :, g]
                o, l = dilated_group_prompt(q_p[:, :, g], kvg_p, dil, win // dil)
                outs_p.append(o)
                lses_p.append(l)
                kv_all = jnp.concatenate([b_caches[g][j], kvg_s.astype(b_caches[g].dtype)], axis=1)
                o, l = dilated_group_sample(q_s[:, :, g], kv_all, dil, win // dil)
                outs_s.append(o)
                lses_s.append(l)
                b_p[g].append(kvg_p[:, S - min(win, S):])
                b_s[g].append(kvg_s)
            mp = dilated_merge(outs_p, lses_p, b_w_o[j])
            ms = dilated_merge(outs_s, lses_s, b_w_o[j])
        elif m == 2:
            q_p, kv_p, lf_p = fox_project(hp, c_w_in[j], c_b_f[j])
            q_s, kv_s, lf_s = fox_project(hs, c_w_in[j], c_b_f[j])
            kv_all = jnp.concatenate([gather_pages(cache_c_kv, j, page_table), kv_s.astype(cache_c_kv.dtype)], axis=1)
            lf_all = jnp.concatenate([gather_pages(cache_c_logf, j, page_table).astype(jnp.float32), lf_s], axis=1)
            mp = fox_attend(q_p, pos_p, kv_p, lf_p, pos_p, c_w_o[j])
            ms = fox_attend(q_s, pos_s, kv_all, lf_all, kpos_s, c_w_o[j])
            ckv_p.append(kv_p)
            ckv_s.append(kv_s)
            clf_p.append(lf_p)
            clf_s.append(lf_s)
        else:
            q_p, kv_p = sb_project(hp, d_w_in[j])
            q_s, kv_s = sb_project(hs, d_w_in[j])
            kv_all = jnp.concatenate([gather_pages(cache_d_kv, j, page_table), kv_s.astype(cache_d_kv.dtype)], axis=1)
            mp = sb_attend(q_p, pos_p, kv_p, pos_p, d_w_o[j])
            ms = sb_attend(q_s, pos_s, kv_all, kpos_s, d_w_o[j])
            dkv_p.append(kv_p)
            dkv_s.append(kv_s)
        xp = xp + mp.astype(xp.dtype)
        xs = xs + ms.astype(xs.dtype)
        xp = macaron_half(xp, ffn_norm_g[i, 1], ffn_w_gu[i, 1], ffn_w_down[i, 1])
        xs = macaron_half(xs, ffn_norm_g[i, 1], ffn_w_gu[i, 1], ffn_w_down[i, 1])

    y_prompt = rms_norm(xp, final_norm_g)
    y_sample = rms_norm(xs, final_norm_g)
    new_a_latent_p = jnp.stack(a_p)
    new_a_latent_s = jnp.stack(a_s)
    new_b_w128_p = jnp.stack(b_p[0])
    new_b_w128_s = jnp.stack(b_s[0])
    new_b_w512_p = jnp.stack(b_p[1])
    new_b_w512_s = jnp.stack(b_s[1])
    new_b_w2048_p = jnp.stack(b_p[2])
    new_b_w2048_s = jnp.stack(b_s[2])
    new_c_kv_p = jnp.stack(ckv_p)
    new_c_kv_s = jnp.stack(ckv_s)
    new_c_logf_p = jnp.stack(clf_p)
    new_c_logf_s = jnp.stack(clf_s)
    new_d_kv_p = jnp.stack(dkv_p)
    new_d_kv_s = jnp.stack(dkv_s)
    return (y_prompt, y_sample, new_a_latent_p, new_a_latent_s,
            new_b_w128_p, new_b_w128_s, new_b_w512_p, new_b_w512_s, new_b_w2048_p, new_b_w2048_s,
            new_c_kv_p, new_c_kv_s, new_c_logf_p, new_c_logf_s, new_d_kv_p, new_d_kv_s)
```

```python
import functools

import jax
import jax.numpy as jnp
from jax import lax
from jax.experimental import pallas as pl
from jax.experimental.pallas import tpu as pltpu

D_MODEL = 1024
PAGE_SIZE = 128
N_MIXERS = 4
HEAD_DIM = 128
N_HEADS = D_MODEL // HEAD_DIM
ROPE_THETA = 500000.0
ROT_DIM = HEAD_DIM // 4
Q_BLOCK = 128
RMS_EPS = 1e-6
D_FF = 2816

MLA_HEADS = N_HEADS
MLA_NOPE = 128
MLA_ROPE = 64
MLA_V = 128
MLA_Q_LORA = 384
MLA_KV_LORA = 256
DIL_PATTERNS = ((128, 1), (512, 4), (2048, 16))
DIL_GROUPS = 3
DIL_HEADS = N_HEADS
FOX_HEADS = N_HEADS
FOX_KV_HEADS = 2
SB_HEADS = N_HEADS
SB_KV_HEADS = 2

V7X_VMEM_LIMIT_BYTES = 56 * 1024 * 1024


def _ffn_kernel(x_ref, g_ref, wgu_ref, wd_ref, o_ref):
    x = x_ref[...]
    ms = jnp.mean(x * x, axis=-1, keepdims=True)
    h = (x * lax.rsqrt(ms + RMS_EPS) * g_ref[...]).astype(jnp.bfloat16)
    gu = jnp.dot(h, wgu_ref[...], preferred_element_type=jnp.float32)
    gate = gu[:, :D_FF]
    up = gu[:, D_FF:]
    act = (gate / (1.0 + jnp.exp(-gate)) * up).astype(jnp.bfloat16)
    o_ref[...] = x + 0.5 * jnp.dot(act, wd_ref[...], preferred_element_type=jnp.float32)


def ffn_half(x2d, g, w_gu, w_down):
    m = x2d.shape[0]
    tm = min(m, 512)
    assert m % tm == 0
    resident = pl.Buffered(1)
    return pl.pallas_call(
        _ffn_kernel,
        grid=(m // tm,),
        in_specs=[
            pl.BlockSpec((tm, D_MODEL), lambda i: (i, 0)),
            pl.BlockSpec((1, D_MODEL), lambda i: (0, 0), pipeline_mode=resident),
            pl.BlockSpec((D_MODEL, 2 * D_FF), lambda i: (0, 0), pipeline_mode=resident),
            pl.BlockSpec((D_FF, D_MODEL), lambda i: (0, 0), pipeline_mode=resident),
        ],
        out_specs=pl.BlockSpec((tm, D_MODEL), lambda i: (i, 0)),
        out_shape=jax.ShapeDtypeStruct((m, D_MODEL), jnp.float32),
        compiler_params=pltpu.CompilerParams(
            dimension_semantics=("arbitrary",), vmem_limit_bytes=V7X_VMEM_LIMIT_BYTES),
        name="ffn_half",
    )(x2d, g.reshape(1, D_MODEL), w_gu.astype(jnp.bfloat16), w_down.astype(jnp.bfloat16))


def macaron_half(x, g, w_gu, w_down):
    b, t, d = x.shape
    return ffn_half(x.reshape(b * t, d), g, w_gu, w_down).reshape(b, t, d)


def rms_norm(x, g):
    xf = x.astype(jnp.float32)
    y = xf * lax.rsqrt(jnp.mean(xf * xf, axis=-1, keepdims=True) + RMS_EPS)
    return (y * g.astype(jnp.float32)).astype(x.dtype)


def apply_rope(x, pos, rot_dim):
    half = rot_dim // 2
    inv_freq = jnp.power(ROPE_THETA, -jnp.arange(half, dtype=jnp.float32) * 2.0 / rot_dim)
    ang = pos.astype(jnp.float32)[:, None] * inv_freq[None, :]
    cos = jnp.cos(ang)[None, :, None, :]
    sin = jnp.sin(ang)[None, :, None, :]
    xr = x[..., :rot_dim].astype(jnp.float32)
    x1, x2 = xr[..., :half], xr[..., half:]
    rot = jnp.concatenate([x1 * cos - x2 * sin, x2 * cos + x1 * sin], axis=-1)
    return jnp.concatenate([rot.astype(x.dtype), x[..., rot_dim:]], axis=-1)


def gather_pages(pool, layer, page_table):
    g = pool[layer, page_table]
    return g.reshape(g.shape[0], g.shape[1] * g.shape[2], *g.shape[3:])


def sweep_q_blocks(fn, qs, q_pos):
    T = q_pos.shape[0]
    if T > Q_BLOCK and T % Q_BLOCK == 0:
        nb = T // Q_BLOCK
        qb = tuple(jnp.swapaxes(a.reshape(a.shape[0], nb, Q_BLOCK, *a.shape[2:]), 0, 1) for a in qs)
        pb = q_pos.reshape(nb, Q_BLOCK)
        out = lax.map(lambda args: fn(*args[0], args[1]), (qb, pb))
        out = jnp.swapaxes(out, 0, 1)
        return out.reshape(out.shape[0], T, *out.shape[3:])
    return fn(*qs, q_pos)


def mla_project(h, pos, w_down, q_norm_g, kv_norm_g, w_uq):
    B, T, _ = h.shape
    d = h @ w_down
    c_q = rms_norm(d[..., :MLA_Q_LORA], q_norm_g)
    c_kv = rms_norm(d[..., MLA_Q_LORA:MLA_Q_LORA + MLA_KV_LORA], kv_norm_g)
    k_rope = apply_rope(d[:, :, None, MLA_Q_LORA + MLA_KV_LORA:], pos, MLA_ROPE)[:, :, 0]
    q = (c_q @ w_uq).reshape(B, T, MLA_HEADS, MLA_NOPE + MLA_ROPE)
    q_nope = q[..., :MLA_NOPE]
    q_rope = apply_rope(q[..., MLA_NOPE:], pos, MLA_ROPE)
    latent = jnp.concatenate([c_kv, k_rope], axis=-1)
    return q_nope, q_rope, latent


def mla_attend(q_nope, q_rope, q_pos, latent, k_pos, w_uk, w_uv, w_o):
    B, T = q_nope.shape[:2]
    c_kv = latent[..., :MLA_KV_LORA]
    k_rope = latent[..., MLA_KV_LORA:]
    q_lat = jnp.einsum('bthn,chn->bthc', q_nope, w_uk)
    scale = (MLA_NOPE + MLA_ROPE) ** -0.5

    def block(ql, qr, qp):
        s = (jnp.einsum('bthc,bsc->bhts', ql, c_kv)
             + jnp.einsum('bthr,bsr->bhts', qr, k_rope)).astype(jnp.float32) * scale
        s = jnp.where(k_pos[None, :] <= qp[:, None], s, -jnp.inf)
        p = jax.nn.softmax(s, axis=-1).astype(c_kv.dtype)
        return jnp.einsum('bhts,bsc->bthc', p, c_kv)

    o_lat = sweep_q_blocks(block, (q_lat, q_rope), q_pos)
    o = jnp.einsum('bthc,chd->bthd', o_lat, w_uv).reshape(B, T, MLA_HEADS * MLA_V)
    return o @ w_o


def dilated_project(h, pos, w_in):
    B, T, _ = h.shape
    qkv = (h @ w_in).reshape(B, T, 3, DIL_GROUPS * DIL_HEADS, HEAD_DIM)
    q = apply_rope(qkv[:, :, 0], pos, ROT_DIM).reshape(B, T, DIL_GROUPS, DIL_HEADS, HEAD_DIM)
    k = apply_rope(qkv[:, :, 1], pos, ROT_DIM).reshape(B, T, DIL_GROUPS, DIL_HEADS, HEAD_DIM)
    v = qkv[:, :, 2].reshape(B, T, DIL_GROUPS, DIL_HEADS, HEAD_DIM)
    return q, jnp.stack([k, v], axis=2)


def dilated_group_prompt(q, kv, dil, n):
    B, S, H, D = q.shape
    L = S // dil
    nb = -(-L // n)
    Lp = nb * n

    def to_sub(a):
        a = a.reshape(B, L, dil, H, D).transpose(0, 2, 1, 3, 4)
        return jnp.pad(a, ((0, 0), (0, 0), (0, Lp - L), (0, 0), (0, 0)))

    def window(a):
        a = jnp.pad(a, ((0, 0), (0, 0), (n, 0), (0, 0), (0, 0))).reshape(B, dil, nb + 1, n, H, D)
        return jnp.concatenate([a[:, :, :-1], a[:, :, 1:]], axis=3)

    qb = to_sub(q).reshape(B, dil, nb, n, H, D)
    kw = window(to_sub(kv[:, :, 0]))
    vw = window(to_sub(kv[:, :, 1]))
    s = jnp.einsum('brnqhd,brnkhd->brnhqk', qb, kw).astype(jnp.float32) * (HEAD_DIM ** -0.5)
    qi = jnp.arange(n)[:, None]
    kj = jnp.arange(2 * n)[None, :]
    dist = qi + n - kj
    key_idx = jnp.arange(nb)[:, None, None] * n - n + kj[None]
    mask = (dist >= 0) & (dist <= n) & (key_idx >= 0)
    s = jnp.where(mask[None, None, :, None], s, -jnp.inf)
    lse = jax.nn.logsumexp(s, axis=-1)
    p = jnp.exp(s - lse[..., None]).astype(q.dtype)
    o = jnp.einsum('brnhqk,brnkhd->brnqhd', p, vw)
    o = o.reshape(B, dil, Lp, H, D)[:, :, :L].transpose(0, 2, 1, 3, 4).reshape(B, S, H, D)
    lse = lse.transpose(0, 1, 2, 4, 3).reshape(B, dil, Lp, H)[:, :, :L].transpose(0, 2, 1, 3).reshape(B, S, H)
    return o, lse


def dilated_group_sample(q, kv_all, dil, n):
    B, T = q.shape[:2]
    L_buf = kv_all.shape[1] - T
    idx = L_buf + jnp.arange(T)[:, None] - dil * jnp.arange(n + 1)[None, :]
    valid = idx >= 0
    g = kv_all[:, jnp.maximum(idx, 0)]
    s = jnp.einsum('bthd,btkhd->bthk', q, g[:, :, :, 0]).astype(jnp.float32) * (HEAD_DIM ** -0.5)
    s = jnp.where(valid[None, :, None, :], s, -jnp.inf)
    lse = jax.nn.logsumexp(s, axis=-1)
    p = jnp.exp(s - lse[..., None]).astype(q.dtype)
    o = jnp.einsum('bthk,btkhd->bthd', p, g[:, :, :, 1])
    return o, lse


def dilated_merge(outs, lses, w_o):
    o = jnp.stack(outs, axis=0)
    wts = jax.nn.softmax(jnp.stack(lses, axis=0), axis=0).astype(o.dtype)
    o = jnp.einsum('gbth,gbthd->bthd', wts, o)
    return o.reshape(o.shape[0], o.shape[1], DIL_HEADS * HEAD_DIM) @ w_o


def fox_project(h, w_in, b_f):
    B, T, _ = h.shape
    nq, nkv = FOX_HEADS * HEAD_DIM, FOX_KV_HEADS * HEAD_DIM
    p = h @ w_in
    q = p[..., :nq].reshape(B, T, FOX_HEADS, HEAD_DIM)
    kv = p[..., nq:nq + 2 * nkv].reshape(B, T, 2, FOX_KV_HEADS, HEAD_DIM)
    logf = jax.nn.log_sigmoid((p[..., nq + 2 * nkv:] + b_f).astype(jnp.float32))
    return q, kv, logf


def fox_attend(q, q_pos, kv_all, logf_all, k_pos, w_o):
    B, T = q.shape[:2]
    S = kv_all.shape[1]
    rep = FOX_HEADS // FOX_KV_HEADS
    c = jnp.cumsum(logf_all.astype(jnp.float32), axis=1)
    c_q = c[:, S - T:]
    c_k = c.reshape(B, S, FOX_KV_HEADS, rep).transpose(0, 2, 3, 1)
    k, v = kv_all[:, :, 0], kv_all[:, :, 1]
    qg = q.reshape(B, T, FOX_KV_HEADS, rep, HEAD_DIM)
    scale = HEAD_DIM ** -0.5

    def block(qb, cqb, qp):
        s = jnp.einsum('btgrd,bsgd->bgrts', qb, k).astype(jnp.float32) * scale
        cq = cqb.reshape(B, -1, FOX_KV_HEADS, rep).transpose(0, 2, 3, 1)[..., None]
        s = s + cq - c_k[:, :, :, None, :]
        s = jnp.where(k_pos[None, :] <= qp[:, None], s, -jnp.inf)
        p = jax.nn.softmax(s, axis=-1).astype(v.dtype)
        return jnp.einsum('bgrts,bsgd->btgrd', p, v)

    o = sweep_q_blocks(block, (qg, c_q), q_pos)
    return o.reshape(B, T, FOX_HEADS * HEAD_DIM) @ w_o


def sb_project(h, w_in):
    B, T, _ = h.shape
    nq, nkv = SB_HEADS * HEAD_DIM, SB_KV_HEADS * HEAD_DIM
    p = h @ w_in
    q = p[..., :nq].reshape(B, T, SB_HEADS, HEAD_DIM)
    kv = p[..., nq:nq + 2 * nkv].reshape(B, T, 2, SB_KV_HEADS, HEAD_DIM)
    return q, kv


def sb_attend(q, q_pos, kv_all, k_pos, w_o):
    B, T = q.shape[:2]
    rep = SB_HEADS // SB_KV_HEADS
    k, v = kv_all[:, :, 0], kv_all[:, :, 1]
    qg = q.reshape(B, T, SB_KV_HEADS, rep, HEAD_DIM)
    scale = HEAD_DIM ** -0.5

    def block(qb, qp):
        z = jnp.einsum('btgrd,bsgd->bgrts', qb, k).astype(jnp.float32) * scale
        mask = k_pos[None, :] < qp[:, None]
        log_keep = jnp.where(mask, jax.nn.log_sigmoid(-z), 0.0)
        later = lax.cumsum(log_keep, axis=z.ndim - 1, reverse=True) - log_keep
        a = jnp.where(mask, jnp.exp(jax.nn.log_sigmoid(z) + later), 0.0).astype(v.dtype)
        return jnp.einsum('bgrts,bsgd->btgrd', a, v)

    o = sweep_q_blocks(block, (qg,), q_pos)
    return o.reshape(B, T, SB_HEADS * HEAD_DIM) @ w_o


def kernel(x_prompt, x_sample, cache_a_latent, cache_b_kv_w128, cache_b_kv_w512, cache_b_kv_w2048,
           cache_c_kv, cache_c_logf, cache_d_kv, page_table,
           ffn_norm_g, ffn_w_gu, ffn_w_down, mix_norm_g,
           a_w_down, a_q_norm_g, a_kv_norm_g, a_w_uq, a_w_uk, a_w_uv, a_w_o,
           b_w_in, b_w_o, c_w_in, c_b_f, c_w_o, d_w_in, d_w_o, final_norm_g):
    DEPTH = ffn_norm_g.shape[0]
    S = x_prompt.shape[1]
    T = x_sample.shape[1]
    P = page_table.shape[1] * PAGE_SIZE
    pos_p = jnp.arange(S, dtype=jnp.int32)
    pos_s = P + jnp.arange(T, dtype=jnp.int32)
    kpos_s = jnp.arange(P + T, dtype=jnp.int32)
    b_caches = (cache_b_kv_w128, cache_b_kv_w512, cache_b_kv_w2048)

    a_p, a_s = [], []
    b_p = [[] for _ in range(DIL_GROUPS)]
    b_s = [[] for _ in range(DIL_GROUPS)]
    ckv_p, ckv_s, clf_p, clf_s = [], [], [], []
    dkv_p, dkv_s = [], []

    xp, xs = x_prompt, x_sample
    for i in range(DEPTH):
        m, j = i % N_MIXERS, i // N_MIXERS
        xp = macaron_half(xp, ffn_norm_g[i, 0], ffn_w_gu[i, 0], ffn_w_down[i, 0])
        xs = macaron_half(xs, ffn_norm_g[i, 0], ffn_w_gu[i, 0], ffn_w_down[i, 0])
        hp = rms_norm(xp, mix_norm_g[i])
        hs = rms_norm(xs, mix_norm_g[i])
        if m == 0:
            qn_p, qr_p, lat_p = mla_project(hp, pos_p, a_w_down[j], a_q_norm_g[j], a_kv_norm_g[j], a_w_uq[j])
            qn_s, qr_s, lat_s = mla_project(hs, pos_s, a_w_down[j], a_q_norm_g[j], a_kv_norm_g[j], a_w_uq[j])
            lat_all = jnp.concatenate([gather_pages(cache_a_latent, j, page_table), lat_s.astype(cache_a_latent.dtype)], axis=1)
            mp = mla_attend(qn_p, qr_p, pos_p, lat_p, pos_p, a_w_uk[j], a_w_uv[j], a_w_o[j])
            ms = mla_attend(qn_s, qr_s, pos_s, lat_all, kpos_s, a_w_uk[j], a_w_uv[j], a_w_o[j])
            a_p.append(lat_p)
            a_s.append(lat_s)
        elif m == 1:
            q_p, kv_p = dilated_project(hp, pos_p, b_w_in[j])
            q_s, kv_s = dilated_project(hs, pos_s, b_w_in[j])
            outs_p, lses_p, outs_s, lses_s = [], [], [], []
            for g, (win, dil) in enumerate(DIL_PATTERNS):
                kvg_p = kv_p[:, :, :, g]
                kvg_s = kv_s[:, :, :, g]
                o, l = dilated_group_prompt(q_p[:, :, g], kvg_p, dil, win // dil)
                outs_p.append(o)
                lses_p.append(l)
                kv_all = jnp.concatenate([b_caches[g][j], kvg_s.astype(b_caches[g].dtype)], axis=1)
                o, l = dilated_group_sample(q_s[:, :, g], kv_all, dil, win // dil)
                outs_s.append(o)
                lses_s.append(l)
                b_p[g].append(kvg_p[:, S - min(win, S):])
                b_s[g].append(kvg_s)
            mp = dilated_merge(outs_p, lses_p, b_w_o[j])
            ms = dilated_merge(outs_s, lses_s, b_w_o[j])
        elif m == 2:
            q_p, kv_p, lf_p = fox_project(hp, c_w_in[j], c_b_f[j])
            q_s, kv_s, lf_s = fox_project(hs, c_w_in[j], c_b_f[j])
            kv_all = jnp.concatenate([gather_pages(cache_c_kv, j, page_table), kv_s.astype(cache_c_kv.dtype)], axis=1)
            lf_all = jnp.concatenate([gather_pages(cache_c_logf, j, page_table).astype(jnp.float32), lf_s], axis=1)
            mp = fox_attend(q_p, pos_p, kv_p, lf_p, pos_p, c_w_o[j])
            ms = fox_attend(q_s, pos_s, kv_all, lf_all, kpos_s, c_w_o[j])
            ckv_p.append(kv_p)
            ckv_s.append(kv_s)
            clf_p.append(lf_p)
            clf_s.append(lf_s)
        else:
            q_p, kv_p = sb_project(hp, d_w_in[j])
            q_s, kv_s = sb_project(hs, d_w_in[j])
            kv_all = jnp.concatenate([gather_pages(cache_d_kv, j, page_table), kv_s.astype(cache_d_kv.dtype)], axis=1)
            mp = sb_attend(q_p, pos_p, kv_p, pos_p, d_w_o[j])
            ms = sb_attend(q_s, pos_s, kv_all, kpos_s, d_w_o[j])
            dkv_p.append(kv_p)
            dkv_s.append(kv_s)
        xp = xp + mp.astype(xp.dtype)
        xs = xs + ms.astype(xs.dtype)
        xp = macaron_half(xp, ffn_norm_g[i, 1], ffn_w_gu[i, 1], ffn_w_down[i, 1])
        xs = macaron_half(xs, ffn_norm_g[i, 1], ffn_w_gu[i, 1], ffn_w_down[i, 1])

    y_prompt = rms_norm(xp, final_norm_g)
    y_sample = rms_norm(xs, final_norm_g)
    return (y_prompt, y_sample, jnp.stack(a_p), jnp.stack(a_s),
            jnp.stack(b_p[0]), jnp.stack(b_s[0]), jnp.stack(b_p[1]), jnp.stack(b_s[1]),
            jnp.stack(b_p[2]), jnp.stack(b_s[2]),
            jnp.stack(ckv_p), jnp.stack(ckv_s), jnp.stack(clf_p), jnp.stack(clf_s),
            jnp.stack(dkv_p), jnp.stack(dkv_s))
```

```python
import functools

import jax
import jax.numpy as jnp
from jax import lax
from jax.experimental import pallas as pl
from jax.experimental.pallas import tpu as pltpu

D_MODEL = 1024
PAGE_SIZE = 128
N_MIXERS = 4
HEAD_DIM = 128
N_HEADS = D_MODEL // HEAD_DIM
ROPE_THETA = 500000.0
ROT_DIM = HEAD_DIM // 4
Q_BLOCK = 128
RMS_EPS = 1e-6
D_FF = 2816

MLA_HEADS = N_HEADS
MLA_NOPE = 128
MLA_ROPE = 64
MLA_V = 128
MLA_Q_LORA = 384
MLA_KV_LORA = 256
DIL_PATTERNS = ((128, 1), (512, 4), (2048, 16))
DIL_GROUPS = 3
DIL_HEADS = N_HEADS
FOX_HEADS = N_HEADS
FOX_KV_HEADS = 2
SB_HEADS = N_HEADS
SB_KV_HEADS = 2

V7X_VMEM_LIMIT_BYTES = 56 * 1024 * 1024


def _ffn_kernel(x_ref, g_ref, wgu_ref, wd_ref, o_ref):
    x = x_ref[...]
    ms = jnp.mean(x * x, axis=-1, keepdims=True)
    h = (x * lax.rsqrt(ms + RMS_EPS) * g_ref[...]).astype(jnp.bfloat16)
    gu = jnp.dot(h, wgu_ref[...], preferred_element_type=jnp.float32)
    gate = gu[:, :D_FF]
    up = gu[:, D_FF:]
    act = (gate / (1.0 + jnp.exp(-gate)) * up).astype(jnp.bfloat16)
    o_ref[...] = x + 0.5 * jnp.dot(act, wd_ref[...], preferred_element_type=jnp.float32)


def ffn_half(x2d, g, w_gu, w_down):
    m = x2d.shape[0]
    tm = min(m, 512)
    assert m % tm == 0
    resident = pl.Buffered(1)
    return pl.pallas_call(
        _ffn_kernel,
        grid=(m // tm,),
        in_specs=[
            pl.BlockSpec((tm, D_MODEL), lambda i: (i, 0)),
            pl.BlockSpec((1, D_MODEL), lambda i: (0, 0), pipeline_mode=resident),
            pl.BlockSpec((D_MODEL, 2 * D_FF), lambda i: (0, 0), pipeline_mode=resident),
            pl.BlockSpec((D_FF, D_MODEL), lambda i: (0, 0), pipeline_mode=resident),
        ],
        out_specs=pl.BlockSpec((tm, D_MODEL), lambda i: (i, 0)),
        out_shape=jax.ShapeDtypeStruct((m, D_MODEL), jnp.float32),
        compiler_params=pltpu.CompilerParams(
            dimension_semantics=("arbitrary",), vmem_limit_bytes=V7X_VMEM_LIMIT_BYTES),
        name="ffn_half",
    )(x2d, g.reshape(1, D_MODEL), w_gu.astype(jnp.bfloat16), w_down.astype(jnp.bfloat16))


def macaron_half(x, g, w_gu, w_down):
    b, t, d = x.shape
    return ffn_half(x.reshape(b * t, d), g, w_gu, w_down).reshape(b, t, d)


def _linear_kernel(*refs, has_norm, has_res):
    refs = list(refs)
    x_ref = refs.pop(0)
    g_ref = refs.pop(0) if has_norm else None
    w_ref = refs.pop(0)
    r_ref = refs.pop(0) if has_res else None
    o_ref = refs.pop(0)
    x = x_ref[...]
    if has_norm:
        ms = jnp.mean(x * x, axis=-1, keepdims=True)
        x = x * lax.rsqrt(ms + RMS_EPS) * g_ref[...]
    y = jnp.dot(x.astype(jnp.bfloat16), w_ref[...], preferred_element_type=jnp.float32)
    if has_res:
        y = r_ref[...] + y
    o_ref[...] = y


def _pick_tile(n, cap):
    if n <= cap:
        return n
    best = n
    for t in range(128, cap + 1, 128):
        if n % t == 0:
            best = t
    return best


def linear(x2d, w, norm_g=None, residual=None):
    m, kdim = x2d.shape
    n = w.shape[1]
    tm = _pick_tile(m, 512)
    tn = _pick_tile(n, 2304)
    assert m % tm == 0 and n % tn == 0
    in_specs = [pl.BlockSpec((tm, kdim), lambda jn, im: (im, 0))]
    args = [x2d]
    if norm_g is not None:
        in_specs.append(pl.BlockSpec((1, kdim), lambda jn, im: (0, 0)))
        args.append(norm_g.reshape(1, kdim))
    in_specs.append(pl.BlockSpec((kdim, tn), lambda jn, im: (0, jn)))
    args.append(w.astype(jnp.bfloat16))
    if residual is not None:
        in_specs.append(pl.BlockSpec((tm, tn), lambda jn, im: (im, jn)))
        args.append(residual)
    return pl.pallas_call(
        functools.partial(_linear_kernel, has_norm=norm_g is not None, has_res=residual is not None),
        grid=(n // tn, m // tm),
        in_specs=in_specs,
        out_specs=pl.BlockSpec((tm, tn), lambda jn, im: (im, jn)),
        out_shape=jax.ShapeDtypeStruct((m, n), jnp.float32),
        compiler_params=pltpu.CompilerParams(
            dimension_semantics=("arbitrary", "arbitrary"), vmem_limit_bytes=V7X_VMEM_LIMIT_BYTES),
        name="linear",
    )(*args)


def _headwise_kernel(x_ref, w_ref, o_ref, *, n_heads, din, dout):
    for h in range(n_heads):
        xh = x_ref[:, h * din:(h + 1) * din].astype(jnp.bfloat16)
        o_ref[:, h * dout:(h + 1) * dout] = jnp.dot(xh, w_ref[h], preferred_element_type=jnp.float32)


def headwise_matmul(x2d, w):
    m = x2d.shape[0]
    n_heads, din, dout = w.shape
    tm = _pick_tile(m, 512)
    return pl.pallas_call(
        functools.partial(_headwise_kernel, n_heads=n_heads, din=din, dout=dout),
        grid=(m // tm,),
        in_specs=[pl.BlockSpec((tm, n_heads * din), lambda i: (i, 0)),
                  pl.BlockSpec((n_heads, din, dout), lambda i: (0, 0, 0))],
        out_specs=pl.BlockSpec((tm, n_heads * dout), lambda i: (i, 0)),
        out_shape=jax.ShapeDtypeStruct((m, n_heads * dout), jnp.float32),
        compiler_params=pltpu.CompilerParams(
            dimension_semantics=("arbitrary",), vmem_limit_bytes=V7X_VMEM_LIMIT_BYTES),
        name="headwise_matmul",
    )(x2d, w.astype(jnp.bfloat16))


ATTN_BLOCK = 256


def _causal_mask(tq, tk, strict):
    row = lax.broadcasted_iota(jnp.int32, (tq, tk), 0)
    col = lax.broadcasted_iota(jnp.int32, (tq, tk), 1)
    return (col < row) if strict else (col <= row)


def _dot_nt(a, b):
    return lax.dot_general(a, b, (((1,), (1,)), ((), ())), preferred_element_type=jnp.float32)


def _softmax_heads(score_fn, v_ref, o_ref, n_heads, dv, blk):
    qi = pl.program_id(2)
    mask = _causal_mask(blk, blk, strict=False)

    def diagonal(h):
        s = jnp.where(mask, score_fn(h, qi), -1e30)
        m0 = jnp.max(s, axis=-1, keepdims=True)
        p = jnp.exp(s - m0)
        l0 = jnp.sum(p, axis=-1, keepdims=True)
        vd = v_ref[pl.ds(pl.multiple_of(qi * blk, blk), blk), :]
        return m0, l0, jnp.dot(p.astype(jnp.bfloat16), vd, preferred_element_type=jnp.float32)

    def past_block(h, kb, carry):
        m, l, acc = carry
        s = score_fn(h, kb)
        m_new = jnp.maximum(m, jnp.max(s, axis=-1, keepdims=True))
        alpha = jnp.exp(m - m_new)
        p = jnp.exp(s - m_new)
        l = alpha * l + jnp.sum(p, axis=-1, keepdims=True)
        vb = v_ref[pl.ds(pl.multiple_of(kb * blk, blk), blk), :]
        acc = alpha * acc + jnp.dot(p.astype(jnp.bfloat16), vb, preferred_element_type=jnp.float32)
        return m_new, l, acc

    def body(j, carry):
        return tuple(past_block(h, qi - 1 - j, carry[h]) for h in range(n_heads))

    final = lax.fori_loop(0, qi, body, tuple(diagonal(h) for h in range(n_heads)))
    for h, (m, l, acc) in enumerate(final):
        o_ref[:, h * dv:(h + 1) * dv] = acc / l


def _fox_prompt_kernel(q_ref, k_ref, v_ref, cq_ref, ck_ref, o_ref, kb_ref, vb_ref, *, rep, blk):
    qi = pl.program_id(2)

    @pl.when(qi == 0)
    def _():
        kb_ref[...] = k_ref[...].astype(jnp.bfloat16)
        vb_ref[...] = v_ref[...].astype(jnp.bfloat16)

    scale = HEAD_DIM ** -0.5
    qs = [(q_ref[:, h * HEAD_DIM:(h + 1) * HEAD_DIM] * scale).astype(jnp.bfloat16) for h in range(rep)]

    def score(h, kb):
        start = pl.multiple_of(kb * blk, blk)
        kblk = kb_ref[pl.ds(start, blk), :]
        cq = cq_ref[:, h:h + 1]
        ck = ck_ref[pl.ds(h, 1), pl.ds(start, blk)]
        return _dot_nt(qs[h], kblk) + cq - ck

    _softmax_heads(score, vb_ref, o_ref, rep, HEAD_DIM, blk)


def fox_attention_prompt(p, c):
    b, t, _ = p.shape
    blk = min(ATTN_BLOCK, t)
    rep = FOX_HEADS // FOX_KV_HEADS
    nqb = FOX_HEADS
    ck = jnp.transpose(c, (0, 2, 1)).reshape(b, FOX_KV_HEADS, rep, t)
    cq = jnp.transpose(c.reshape(b, t, FOX_KV_HEADS, rep), (0, 2, 1, 3))
    return pl.pallas_call(
        functools.partial(_fox_prompt_kernel, rep=rep, blk=blk),
        grid=(b, FOX_KV_HEADS, t // blk),
        in_specs=[
            pl.BlockSpec((None, blk, rep * HEAD_DIM), lambda i, g, q: (i, q, g)),
            pl.BlockSpec((None, t, HEAD_DIM), lambda i, g, q: (i, 0, nqb + g)),
            pl.BlockSpec((None, t, HEAD_DIM), lambda i, g, q: (i, 0, nqb + FOX_KV_HEADS + g)),
            pl.BlockSpec((None, None, blk, rep), lambda i, g, q: (i, g, q, 0)),
            pl.BlockSpec((None, None, rep, t), lambda i, g, q: (i, g, 0, 0)),
        ],
        out_specs=pl.BlockSpec((None, blk, rep * HEAD_DIM), lambda i, g, q: (i, q, g)),
        out_shape=jax.ShapeDtypeStruct((b, t, FOX_HEADS * HEAD_DIM), jnp.float32),
        scratch_shapes=[pltpu.VMEM((t, HEAD_DIM), jnp.bfloat16), pltpu.VMEM((t, HEAD_DIM), jnp.bfloat16)],
        compiler_params=pltpu.CompilerParams(
            dimension_semantics=("arbitrary", "arbitrary", "arbitrary"), vmem_limit_bytes=V7X_VMEM_LIMIT_BYTES),
        name="fox_attention_prompt",
    )(p, p, p, cq, ck)


def _mla_prompt_kernel(ql_ref, qr_ref, lat_ref, o_ref, kb_ref, *, blk):
    qi = pl.program_id(2)

    @pl.when(qi == 0)
    def _():
        kb_ref[...] = lat_ref[...].astype(jnp.bfloat16)

    scale = (MLA_NOPE + MLA_ROPE) ** -0.5
    ql = [(ql_ref[:, h * MLA_KV_LORA:(h + 1) * MLA_KV_LORA] * scale).astype(jnp.bfloat16) for h in range(MLA_HEADS)]
    qr = [(qr_ref[:, h * MLA_ROPE:(h + 1) * MLA_ROPE] * scale).astype(jnp.bfloat16) for h in range(MLA_HEADS)]

    def score(h, kb):
        start = pl.multiple_of(kb * blk, blk)
        k_lat = kb_ref[pl.ds(start, blk), :MLA_KV_LORA]
        k_rope = kb_ref[pl.ds(start, blk), MLA_KV_LORA:]
        return _dot_nt(ql[h], k_lat) + _dot_nt(qr[h], k_rope)

    _softmax_heads(score, kb_ref.at[:, :MLA_KV_LORA], o_ref, MLA_HEADS, MLA_KV_LORA, blk)


def mla_attention_prompt(q_lat, q_rope, latent):
    b, t, _ = q_lat.shape
    blk = min(ATTN_BLOCK, t)
    dl = MLA_KV_LORA + MLA_ROPE
    return pl.pallas_call(
        functools.partial(_mla_prompt_kernel, blk=blk),
        grid=(b, 1, t // blk),
        in_specs=[
            pl.BlockSpec((None, blk, MLA_HEADS * MLA_KV_LORA), lambda i, g, q: (i, q, 0)),
            pl.BlockSpec((None, blk, MLA_HEADS * MLA_ROPE), lambda i, g, q: (i, q, 0)),
            pl.BlockSpec((None, t, dl), lambda i, g, q: (i, 0, 0)),
        ],
        out_specs=pl.BlockSpec((None, blk, MLA_HEADS * MLA_KV_LORA), lambda i, g, q: (i, q, 0)),
        out_shape=jax.ShapeDtypeStruct((b, t, MLA_HEADS * MLA_KV_LORA), jnp.float32),
        scratch_shapes=[pltpu.VMEM((t, dl), jnp.bfloat16)],
        compiler_params=pltpu.CompilerParams(
            dimension_semantics=("arbitrary", "arbitrary", "arbitrary"), vmem_limit_bytes=V7X_VMEM_LIMIT_BYTES),
        name="mla_attention_prompt",
    )(q_lat, q_rope, latent)


def _sb_prompt_kernel(q_ref, k_ref, v_ref, o_ref, kb_ref, vb_ref, *, rep, blk):
    qi = pl.program_id(2)

    @pl.when(qi == 0)
    def _():
        kb_ref[...] = k_ref[...].astype(jnp.bfloat16)
        vb_ref[...] = v_ref[...].astype(jnp.bfloat16)

    scale = HEAD_DIM ** -0.5
    mask = _causal_mask(blk, blk, strict=True)
    row = lax.broadcasted_iota(jnp.int32, (blk, blk), 0)
    col = lax.broadcasted_iota(jnp.int32, (blk, blk), 1)
    later_sel = jnp.where(row > col, 1.0, 0.0).astype(jnp.bfloat16)

    def block(qh, kb, carry, masked):
        start = pl.multiple_of(kb * blk, blk)
        z = _dot_nt(qh, kb_ref[pl.ds(start, blk), :])
        sp = jnp.log(1.0 + jnp.exp(-jnp.abs(z)))
        log_beta = jnp.minimum(z, 0.0) - sp
        log_keep = jnp.minimum(-z, 0.0) - sp
        if masked:
            log_keep = jnp.where(mask, log_keep, 0.0)
        hi = log_keep.astype(jnp.bfloat16)
        lo = (log_keep - hi.astype(jnp.float32)).astype(jnp.bfloat16)
        later = (jnp.dot(hi, later_sel, preferred_element_type=jnp.float32)
                 + jnp.dot(lo, later_sel, preferred_element_type=jnp.float32))
        if carry is not None:
            later = later + carry[0]
        a = jnp.exp(log_beta + later)
        if masked:
            a = jnp.where(mask, a, 0.0)
        pv = jnp.dot(a.astype(jnp.bfloat16), vb_ref[pl.ds(start, blk), :], preferred_element_type=jnp.float32)
        tail = jnp.sum(log_keep, axis=-1, keepdims=True)
        if carry is None:
            return tail, pv
        return carry[0] + tail, carry[1] + pv

    qs = [(q_ref[:, h * HEAD_DIM:(h + 1) * HEAD_DIM] * scale).astype(jnp.bfloat16) for h in range(rep)]

    def body(j, carry):
        return tuple(block(qs[h], qi - 1 - j, carry[h], False) for h in range(rep))

    final = lax.fori_loop(0, qi, body, tuple(block(qs[h], qi, None, True) for h in range(rep)))
    for h, (_, acc) in enumerate(final):
        o_ref[:, h * HEAD_DIM:(h + 1) * HEAD_DIM] = acc


def sb_attention_prompt(p):
    b, t, _ = p.shape
    blk = min(ATTN_BLOCK, t)
    rep = SB_HEADS // SB_KV_HEADS
    nqb = SB_HEADS
    return pl.pallas_call(
        functools.partial(_sb_prompt_kernel, rep=rep, blk=blk),
        grid=(b, SB_KV_HEADS, t // blk),
        in_specs=[
            pl.BlockSpec((None, blk, rep * HEAD_DIM), lambda i, g, q: (i, q, g)),
            pl.BlockSpec((None, t, HEAD_DIM), lambda i, g, q: (i, 0, nqb + g)),
            pl.BlockSpec((None, t, HEAD_DIM), lambda i, g, q: (i, 0, nqb + SB_KV_HEADS + g)),
        ],
        out_specs=pl.BlockSpec((None, blk, rep * HEAD_DIM), lambda i, g, q: (i, q, g)),
        out_shape=jax.ShapeDtypeStruct((b, t, SB_HEADS * HEAD_DIM), jnp.float32),
        scratch_shapes=[pltpu.VMEM((t, HEAD_DIM), jnp.bfloat16), pltpu.VMEM((t, HEAD_DIM), jnp.bfloat16)],
        compiler_params=pltpu.CompilerParams(
            dimension_semantics=("arbitrary", "arbitrary", "arbitrary"), vmem_limit_bytes=V7X_VMEM_LIMIT_BYTES),
        name="sb_attention_prompt",
    )(p, p, p)


DIL_N = 128


def _dilated_prompt_kernel(*refs, t):
    qkv = refs[:9]
    o_ref, og_ref, lse_ref = refs[9:]
    scale = HEAD_DIM ** -0.5
    n = DIL_N
    qi = lax.broadcasted_iota(jnp.int32, (n, 2 * n), 0)
    kj = lax.broadcasted_iota(jnp.int32, (n, 2 * n), 1)
    cur_ok = (kj >= n) & (kj - n <= qi)
    prev_ok = (kj < n) & (kj >= qi)

    for g, (_, dil) in enumerate(DIL_PATTERNS):
        q_ref, k_ref, v_ref = qkv[3 * g:3 * g + 3]
        nb = t // (dil * n)

        def unit(u, carry, q_ref=q_ref, k_ref=k_ref, v_ref=v_ref, dil=dil, nb=nb, g=g):
            r, i = u // nb, u % nb
            cur = r + dil * n * i
            prev = r + dil * n * jnp.maximum(i - 1, 0)
            rows = lambda ref, start: ref[pl.ds(start, n, stride=dil), :]
            qb = (rows(q_ref, cur) * scale).astype(jnp.bfloat16)
            kb = jnp.concatenate([rows(k_ref, prev), rows(k_ref, cur)], axis=0).astype(jnp.bfloat16)
            vb = jnp.concatenate([rows(v_ref, prev), rows(v_ref, cur)], axis=0).astype(jnp.bfloat16)
            s = _dot_nt(qb, kb)
            s = jnp.where(cur_ok | (prev_ok & (i > 0)), s, -1e30)
            m = jnp.max(s, axis=-1, keepdims=True)
            p = jnp.exp(s - m)
            l = jnp.sum(p, axis=-1, keepdims=True)
            o = jnp.dot(p.astype(jnp.bfloat16), vb, preferred_element_type=jnp.float32) / l
            og_ref.at[g][pl.ds(cur, n, stride=dil), :] = o
            lse_ref.at[g][pl.ds(cur, n, stride=dil), :] = jnp.broadcast_to(m + jnp.log(l), (n, HEAD_DIM))
            return carry

        lax.fori_loop(0, t // n, unit, 0)

    lses = [lse_ref[g] for g in range(DIL_GROUPS)]
    top = functools.reduce(jnp.maximum, lses)
    ws = [jnp.exp(x - top) for x in lses]
    den = functools.reduce(lambda a, b: a + b, ws)
    o_ref[...] = functools.reduce(lambda a, b: a + b, [w * og_ref[g] for g, w in enumerate(ws)]) / den


def dilated_attention_prompt(q, k, p):
    b, t, _ = q.shape
    assert all(t % (dil * DIL_N) == 0 for _, dil in DIL_PATTERNS)
    v_col0 = 2 * DIL_GROUPS * DIL_HEADS
    in_specs, args = [], []
    for g in range(DIL_GROUPS):
        col = lambda i, h, g=g: (i, 0, g * DIL_HEADS + h)
        in_specs += [pl.BlockSpec((None, t, HEAD_DIM), col), pl.BlockSpec((None, t, HEAD_DIM), col),
                     pl.BlockSpec((None, t, HEAD_DIM), lambda i, h, g=g: (i, 0, v_col0 + g * DIL_HEADS + h))]
        args += [q, k, p]
    return pl.pallas_call(
        functools.partial(_dilated_prompt_kernel, t=t),
        grid=(b, DIL_HEADS),
        in_specs=in_specs,
        out_specs=pl.BlockSpec((None, t, HEAD_DIM), lambda i, h: (i, 0, h)),
        out_shape=jax.ShapeDtypeStruct((b, t, DIL_HEADS * HEAD_DIM), jnp.float32),
        scratch_shapes=[pltpu.VMEM((DIL_GROUPS, t, HEAD_DIM), jnp.float32),
                        pltpu.VMEM((DIL_GROUPS, t, HEAD_DIM), jnp.float32)],
        compiler_params=pltpu.CompilerParams(
            dimension_semantics=("arbitrary", "arbitrary"), vmem_limit_bytes=V7X_VMEM_LIMIT_BYTES),
        name="dilated_attention_prompt",
    )(*args)


def _dilated_sample_kernel(q_ref, kvn_ref, c0_ref, c1_ref, c2_ref, o_ref):
    scale = HEAD_DIM ** -0.5
    outs, lses = [], []
    for g, c_ref in enumerate((c0_ref, c1_ref, c2_ref)):
        qg = q_ref[g] * scale
        kc, vc = c_ref[:, 0], c_ref[:, 1]
        kn, vn = kvn_ref[0, g], kvn_ref[1, g]
        s_c = jnp.sum(kc * qg[None], axis=-1, keepdims=True)
        s_n = jnp.sum(kn * qg, axis=-1, keepdims=True)
        m = jnp.maximum(jnp.max(s_c, axis=0), s_n)
        p_c = jnp.exp(s_c - m[None])
        p_n = jnp.exp(s_n - m)
        l = jnp.sum(p_c, axis=0) + p_n
        outs.append((jnp.sum(p_c * vc, axis=0) + p_n * vn) / l)
        lses.append(m + jnp.log(l))
    top = functools.reduce(jnp.maximum, lses)
    ws = [jnp.exp(x - top) for x in lses]
    den = functools.reduce(lambda a, b: a + b, ws)
    o_ref[...] = functools.reduce(lambda a, b: a + b, [w * o for w, o in zip(ws, outs)]) / den


def dilated_attention_sample(q, kv_new, caches, layer):
    bd = q.shape[0]
    in_specs = [pl.BlockSpec((None, DIL_GROUPS, DIL_HEADS, HEAD_DIM), lambda i: (i, 0, 0, 0)),
                pl.BlockSpec((None, 2, DIL_GROUPS, DIL_HEADS, HEAD_DIM), lambda i: (i, 0, 0, 0, 0))]
    args = [q, kv_new]
    for (win, dil), cache in zip(DIL_PATTERNS, caches):
        assert cache.shape[2] == win == DIL_N * dil
        args.append(cache.reshape(cache.shape[0], bd, DIL_N, dil, 2, DIL_HEADS, HEAD_DIM))
        in_specs.append(pl.BlockSpec((None, None, DIL_N, None, 2, DIL_HEADS, HEAD_DIM),
                                     lambda i: (layer, i, 0, 0, 0, 0, 0)))
    return pl.pallas_call(
        _dilated_sample_kernel,
        grid=(bd,),
        in_specs=in_specs,
        out_specs=pl.BlockSpec((None, DIL_HEADS, HEAD_DIM), lambda i: (i, 0, 0)),
        out_shape=jax.ShapeDtypeStruct((bd, DIL_HEADS, HEAD_DIM), jnp.float32),
        compiler_params=pltpu.CompilerParams(
            dimension_semantics=("arbitrary",), vmem_limit_bytes=V7X_VMEM_LIMIT_BYTES),
        name="dilated_attention_sample",
    )(*args)


DECODE_PAGES = 16


def _split3(x):
    hi = x.astype(jnp.bfloat16).astype(jnp.float32)
    r = x - hi
    mid = r.astype(jnp.bfloat16).astype(jnp.float32)
    lo = r - mid
    return hi, mid, lo


def _dot_split(xs, w_bf16):
    n = xs[0].shape[0]
    pieces = [piece for x in xs for piece in _split3(x)]
    r = jnp.dot(jnp.concatenate(pieces, axis=0).astype(jnp.bfloat16), w_bf16, preferred_element_type=jnp.float32)
    return [r[3 * i * n:(3 * i + 1) * n] + r[(3 * i + 1) * n:(3 * i + 2) * n] + r[(3 * i + 2) * n:(3 * i + 3) * n]
            for i in range(len(xs))]


def _decode_kernel(pt_ref, *refs, mode, n_pages_step, scale):
    del pt_ref
    np_ = n_pages_step
    if mode == "fox":
        q_ref, kn_ref, vn_ref, lfn_ref = refs[:4]
        rest = refs[4:]
        page_refs, lf_refs, rest = rest[:np_], rest[np_:2 * np_], rest[2 * np_:]
    else:
        q_ref, kn_ref, vn_ref = refs[:3]
        rest = refs[3:]
        page_refs, rest = rest[:np_], rest[np_:]
        lf_refs = None
    o_ref, m_sc, l_sc, acc_sc, r_sc = rest
    c = pl.program_id(1)
    n_chunks = pl.num_programs(1)
    gqa = mode != "mla"
    q = q_ref[...] * scale
    qb = q.astype(jnp.bfloat16)
    n_heads = q.shape[0]
    row = lax.broadcasted_iota(jnp.int32, (n_heads, PAGE_SIZE), 0)
    first_group = row < (n_heads // 2)

    @pl.when(c == 0)
    def _():
        if mode == "sb":
            acc_sc[...] = jnp.zeros_like(acc_sc)
            r_sc[...] = jnp.zeros_like(r_sc)
        else:
            m_sc[...] = jnp.sum(q * kn_ref[...], axis=-1, keepdims=True)
            l_sc[...] = jnp.ones_like(l_sc)
            acc_sc[...] = jnp.broadcast_to(vn_ref[...], acc_sc.shape)
            if mode == "fox":
                r_sc[...] = lfn_ref[...]

    def keys_values(k):
        if gqa:
            pr = page_refs[k]
            kcat = jnp.concatenate([pr[pl.ds(g, PAGE_SIZE, stride=4), :] for g in range(2)], axis=0)
            vcat = jnp.concatenate([pr[pl.ds(2 + g, PAGE_SIZE, stride=4), :] for g in range(2)], axis=0)
            return kcat.astype(jnp.bfloat16), vcat.astype(jnp.bfloat16)
        kb = page_refs[k][...].astype(jnp.bfloat16)
        return kb, kb[:, :MLA_KV_LORA]

    def qk(keys):
        r = _dot_nt(qb, keys)
        if gqa:
            return jnp.where(first_group, r[:, :PAGE_SIZE], r[:, PAGE_SIZE:])
        return r

    def pv_dot(w, vals):
        if gqa:
            w = jnp.concatenate([jnp.where(first_group, w, 0.0), jnp.where(first_group, 0.0, w)], axis=1)
        return jnp.dot(w.astype(jnp.bfloat16), vals, preferred_element_type=jnp.float32)

    if mode in ("fox", "sb"):
        rr = lax.broadcasted_iota(jnp.int32, (PAGE_SIZE, PAGE_SIZE), 0)
        cc = lax.broadcasted_iota(jnp.int32, (PAGE_SIZE, PAGE_SIZE), 1)
        later_sel = jnp.where(rr > cc, 1.0, 0.0).astype(jnp.bfloat16)

    def suffix_terms(per_page):
        within = _dot_split(per_page, later_sel)
        run = r_sc[...]
        out = []
        for k in range(np_):
            out.append(within[k] + run)
            run = run + jnp.sum(per_page[k], axis=-1, keepdims=True)
        r_sc[...] = run
        return out

    kvs = [keys_values(k) for k in range(np_)]
    values = [kv[1] for kv in kvs]
    scores = [qk(kv[0]) for kv in kvs]
    if mode == "fox":
        bias = suffix_terms([lf_refs[k][...] for k in range(np_)])
        scores = [s + b for s, b in zip(scores, bias)]
    if mode == "sb":
        z = jnp.concatenate(scores, axis=1)
        sp = jnp.log(1.0 + jnp.exp(-jnp.abs(z)))
        log_beta = jnp.minimum(z, 0.0) - sp
        log_keep = jnp.minimum(-z, 0.0) - sp
        later = suffix_terms([log_keep[:, k * PAGE_SIZE:(k + 1) * PAGE_SIZE] for k in range(np_)])
        a = jnp.exp(log_beta + jnp.concatenate(later, axis=1))
        pv = None
        for k in range(np_):
            t = pv_dot(a[:, k * PAGE_SIZE:(k + 1) * PAGE_SIZE], values[k])
            pv = t if pv is None else pv + t
        acc_sc[...] = acc_sc[...] + pv
    else:
        s_all = jnp.concatenate(scores, axis=1)
        m_prev = m_sc[...]
        m_new = jnp.maximum(m_prev, jnp.max(s_all, axis=-1, keepdims=True))
        alpha = jnp.exp(m_prev - m_new)
        p_all = jnp.exp(s_all - m_new)
        l_sc[...] = alpha * l_sc[...] + jnp.sum(p_all, axis=-1, keepdims=True)
        pv = None
        for k in range(np_):
            t = pv_dot(p_all[:, k * PAGE_SIZE:(k + 1) * PAGE_SIZE], values[k])
            pv = t if pv is None else pv + t
        acc_sc[...] = alpha * acc_sc[...] + pv
        m_sc[...] = m_new

    @pl.when(c == n_chunks - 1)
    def _():
        if mode == "sb":
            o_ref[...] = acc_sc[...]
        else:
            o_ref[...] = acc_sc[...] / l_sc[...]


def paged_decode(mode, q, k_new, v_new, pool, page_table, layer, lf_new=None, lf_pool_t=None):
    bd, n_heads, dq = q.shape
    dv = v_new.shape[-1]
    n_pages = page_table.shape[1]
    np_ = min(DECODE_PAGES, n_pages)
    assert n_pages % np_ == 0
    scale = (MLA_NOPE + MLA_ROPE) ** -0.5 if mode == "mla" else HEAD_DIM ** -0.5
    if mode != "mla":
        pool = pool.reshape(pool.shape[0], pool.shape[1], PAGE_SIZE * 4, HEAD_DIM)
    page_block = (None, None) + pool.shape[2:]
    zeros = (0,) * (pool.ndim - 2)

    def page_spec(k, block, tail):
        def imap(b, c, pt):
            return (layer, pt[b, n_pages - 1 - (c * np_ + k)]) + tail
        return pl.BlockSpec(block, imap)

    def head_spec(d):
        return pl.BlockSpec((None, n_heads, d), lambda b, c, pt: (b, 0, 0))

    in_specs = [head_spec(dq), head_spec(dq), head_spec(dv)]
    args = [q, k_new, v_new]
    if mode == "fox":
        in_specs.append(head_spec(1))
        args.append(lf_new)
    in_specs += [page_spec(k, page_block, zeros) for k in range(np_)]
    args += [pool] * np_
    if mode == "fox":
        in_specs += [page_spec(k, (None, None, n_heads, PAGE_SIZE), (0, 0)) for k in range(np_)]
        args += [lf_pool_t] * np_
    return pl.pallas_call(
        functools.partial(_decode_kernel, mode=mode, n_pages_step=np_, scale=scale),
        grid_spec=pltpu.PrefetchScalarGridSpec(
            num_scalar_prefetch=1,
            grid=(bd, n_pages // np_),
            in_specs=in_specs,
            out_specs=pl.BlockSpec((None, n_heads, dv), lambda b, c, pt: (b, 0, 0)),
            scratch_shapes=[pltpu.VMEM((n_heads, 1), jnp.float32), pltpu.VMEM((n_heads, 1), jnp.float32),
                            pltpu.VMEM((n_heads, dv), jnp.float32), pltpu.VMEM((n_heads, 1), jnp.float32)],
        ),
        out_shape=jax.ShapeDtypeStruct((bd, n_heads, dv), jnp.float32),
        compiler_params=pltpu.CompilerParams(
            dimension_semantics=("arbitrary", "arbitrary"), vmem_limit_bytes=V7X_VMEM_LIMIT_BYTES),
        name="paged_decode_" + mode,
    )(page_table, *args)


def rms_norm(x, g):
    xf = x.astype(jnp.float32)
    y = xf * lax.rsqrt(jnp.mean(xf * xf, axis=-1, keepdims=True) + RMS_EPS)
    return (y * g.astype(jnp.float32)).astype(x.dtype)


def apply_rope(x, pos, rot_dim):
    half = rot_dim // 2
    inv_freq = jnp.power(ROPE_THETA, -jnp.arange(half, dtype=jnp.float32) * 2.0 / rot_dim)
    ang = pos.astype(jnp.float32)[:, None] * inv_freq[None, :]
    cos = jnp.cos(ang)[None, :, None, :]
    sin = jnp.sin(ang)[None, :, None, :]
    xr = x[..., :rot_dim].astype(jnp.float32)
    x1, x2 = xr[..., :half], xr[..., half:]
    rot = jnp.concatenate([x1 * cos - x2 * sin, x2 * cos + x1 * sin], axis=-1)
    return jnp.concatenate([rot.astype(x.dtype), x[..., rot_dim:]], axis=-1)


def gather_pages(pool, layer, page_table):
    g = pool[layer, page_table]
    return g.reshape(g.shape[0], g.shape[1] * g.shape[2], *g.shape[3:])


def sweep_q_blocks(fn, qs, q_pos):
    T = q_pos.shape[0]
    if T > Q_BLOCK and T % Q_BLOCK == 0:
        nb = T // Q_BLOCK
        qb = tuple(jnp.swapaxes(a.reshape(a.shape[0], nb, Q_BLOCK, *a.shape[2:]), 0, 1) for a in qs)
        pb = q_pos.reshape(nb, Q_BLOCK)
        out = lax.map(lambda args: fn(*args[0], args[1]), (qb, pb))
        out = jnp.swapaxes(out, 0, 1)
        return out.reshape(out.shape[0], T, *out.shape[3:])
    return fn(*qs, q_pos)


def mla_project(h, pos, w_down, q_norm_g, kv_norm_g, w_uq):
    B, T, _ = h.shape
    d = h @ w_down
    c_q = rms_norm(d[..., :MLA_Q_LORA], q_norm_g)
    c_kv = rms_norm(d[..., MLA_Q_LORA:MLA_Q_LORA + MLA_KV_LORA], kv_norm_g)
    k_rope = apply_rope(d[:, :, None, MLA_Q_LORA + MLA_KV_LORA:], pos, MLA_ROPE)[:, :, 0]
    q = (c_q @ w_uq).reshape(B, T, MLA_HEADS, MLA_NOPE + MLA_ROPE)
    q_nope = q[..., :MLA_NOPE]
    q_rope = apply_rope(q[..., MLA_NOPE:], pos, MLA_ROPE)
    latent = jnp.concatenate([c_kv, k_rope], axis=-1)
    return q_nope, q_rope, latent


def mla_attend(q_nope, q_rope, q_pos, latent, k_pos, w_uk, w_uv, w_o):
    B, T = q_nope.shape[:2]
    c_kv = latent[..., :MLA_KV_LORA]
    k_rope = latent[..., MLA_KV_LORA:]
    q_lat = jnp.einsum('bthn,chn->bthc', q_nope, w_uk)
    scale = (MLA_NOPE + MLA_ROPE) ** -0.5

    def block(ql, qr, qp):
        s = (jnp.einsum('bthc,bsc->bhts', ql, c_kv)
             + jnp.einsum('bthr,bsr->bhts', qr, k_rope)).astype(jnp.float32) * scale
        s = jnp.where(k_pos[None, :] <= qp[:, None], s, -jnp.inf)
        p = jax.nn.softmax(s, axis=-1).astype(c_kv.dtype)
        return jnp.einsum('bhts,bsc->bthc', p, c_kv)

    o_lat = sweep_q_blocks(block, (q_lat, q_rope), q_pos)
    o = jnp.einsum('bthc,chd->bthd', o_lat, w_uv).reshape(B, T, MLA_HEADS * MLA_V)
    return o @ w_o


def dilated_project(h, pos, w_in):
    B, T, _ = h.shape
    qkv = (h @ w_in).reshape(B, T, 3, DIL_GROUPS * DIL_HEADS, HEAD_DIM)
    q = apply_rope(qkv[:, :, 0], pos, ROT_DIM).reshape(B, T, DIL_GROUPS, DIL_HEADS, HEAD_DIM)
    k = apply_rope(qkv[:, :, 1], pos, ROT_DIM).reshape(B, T, DIL_GROUPS, DIL_HEADS, HEAD_DIM)
    v = qkv[:, :, 2].reshape(B, T, DIL_GROUPS, DIL_HEADS, HEAD_DIM)
    return q, jnp.stack([k, v], axis=2)


def dilated_group_prompt(q, kv, dil, n):
    B, S, H, D = q.shape
    L = S // dil
    nb = -(-L // n)
    Lp = nb * n

    def to_sub(a):
        a = a.reshape(B, L, dil, H, D).transpose(0, 2, 1, 3, 4)
        return jnp.pad(a, ((0, 0), (0, 0), (0, Lp - L), (0, 0), (0, 0)))

    def window(a):
        a = jnp.pad(a, ((0, 0), (0, 0), (n, 0), (0, 0), (0, 0))).reshape(B, dil, nb + 1, n, H, D)
        return jnp.concatenate([a[:, :, :-1], a[:, :, 1:]], axis=3)

    qb = to_sub(q).reshape(B, dil, nb, n, H, D)
    kw = window(to_sub(kv[:, :, 0]))
    vw = window(to_sub(kv[:, :, 1]))
    s = jnp.einsum('brnqhd,brnkhd->brnhqk', qb, kw).astype(jnp.float32) * (HEAD_DIM ** -0.5)
    qi = jnp.arange(n)[:, None]
    kj = jnp.arange(2 * n)[None, :]
    dist = qi + n - kj
    key_idx = jnp.arange(nb)[:, None, None] * n - n + kj[None]
    mask = (dist >= 0) & (dist <= n) & (key_idx >= 0)
    s = jnp.where(mask[None, None, :, None], s, -jnp.inf)
    lse = jax.nn.logsumexp(s, axis=-1)
    p = jnp.exp(s - lse[..., None]).astype(q.dtype)
    o = jnp.einsum('brnhqk,brnkhd->brnqhd', p, vw)
    o = o.reshape(B, dil, Lp, H, D)[:, :, :L].transpose(0, 2, 1, 3, 4).reshape(B, S, H, D)
    lse = lse.transpose(0, 1, 2, 4, 3).reshape(B, dil, Lp, H)[:, :, :L].transpose(0, 2, 1, 3).reshape(B, S, H)
    return o, lse


def dilated_group_sample(q, kv_all, dil, n):
    B, T = q.shape[:2]
    L_buf = kv_all.shape[1] - T
    idx = L_buf + jnp.arange(T)[:, None] - dil * jnp.arange(n + 1)[None, :]
    valid = idx >= 0
    g = kv_all[:, jnp.maximum(idx, 0)]
    s = jnp.einsum('bthd,btkhd->bthk', q, g[:, :, :, 0]).astype(jnp.float32) * (HEAD_DIM ** -0.5)
    s = jnp.where(valid[None, :, None, :], s, -jnp.inf)
    lse = jax.nn.logsumexp(s, axis=-1)
    p = jnp.exp(s - lse[..., None]).astype(q.dtype)
    o = jnp.einsum('bthk,btkhd->bthd', p, g[:, :, :, 1])
    return o, lse


def dilated_merge(outs, lses, w_o):
    o = jnp.stack(outs, axis=0)
    wts = jax.nn.softmax(jnp.stack(lses, axis=0), axis=0).astype(o.dtype)
    o = jnp.einsum('gbth,gbthd->bthd', wts, o)
    return o.reshape(o.shape[0], o.shape[1], DIL_HEADS * HEAD_DIM) @ w_o


def fox_project(h, w_in, b_f):
    B, T, _ = h.shape
    nq, nkv = FOX_HEADS * HEAD_DIM, FOX_KV_HEADS * HEAD_DIM
    p = h @ w_in
    q = p[..., :nq].reshape(B, T, FOX_HEADS, HEAD_DIM)
    kv = p[..., nq:nq + 2 * nkv].reshape(B, T, 2, FOX_KV_HEADS, HEAD_DIM)
    logf = jax.nn.log_sigmoid((p[..., nq + 2 * nkv:] + b_f).astype(jnp.float32))
    return q, kv, logf


def fox_attend(q, q_pos, kv_all, logf_all, k_pos, w_o):
    B, T = q.shape[:2]
    S = kv_all.shape[1]
    rep = FOX_HEADS // FOX_KV_HEADS
    c = jnp.cumsum(logf_all.astype(jnp.float32), axis=1)
    c_q = c[:, S - T:]
    c_k = c.reshape(B, S, FOX_KV_HEADS, rep).transpose(0, 2, 3, 1)
    k, v = kv_all[:, :, 0], kv_all[:, :, 1]
    qg = q.reshape(B, T, FOX_KV_HEADS, rep, HEAD_DIM)
    scale = HEAD_DIM ** -0.5

    def block(qb, cqb, qp):
        s = jnp.einsum('btgrd,bsgd->bgrts', qb, k).astype(jnp.float32) * scale
        cq = cqb.reshape(B, -1, FOX_KV_HEADS, rep).transpose(0, 2, 3, 1)[..., None]
        s = s + cq - c_k[:, :, :, None, :]
        s = jnp.where(k_pos[None, :] <= qp[:, None], s, -jnp.inf)
        p = jax.nn.softmax(s, axis=-1).astype(v.dtype)
        return jnp.einsum('bgrts,bsgd->btgrd', p, v)

    o = sweep_q_blocks(block, (qg, c_q), q_pos)
    return o.reshape(B, T, FOX_HEADS * HEAD_DIM) @ w_o


def sb_project(h, w_in):
    B, T, _ = h.shape
    nq, nkv = SB_HEADS * HEAD_DIM, SB_KV_HEADS * HEAD_DIM
    p = h @ w_in
    q = p[..., :nq].reshape(B, T, SB_HEADS, HEAD_DIM)
    kv = p[..., nq:nq + 2 * nkv].reshape(B, T, 2, SB_KV_HEADS, HEAD_DIM)
    return q, kv


def sb_attend(q, q_pos, kv_all, k_pos, w_o):
    B, T = q.shape[:2]
    rep = SB_HEADS // SB_KV_HEADS
    k, v = kv_all[:, :, 0], kv_all[:, :, 1]
    qg = q.reshape(B, T, SB_KV_HEADS, rep, HEAD_DIM)
    scale = HEAD_DIM ** -0.5

    def block(qb, qp):
        z = jnp.einsum('btgrd,bsgd->bgrts', qb, k).astype(jnp.float32) * scale
        mask = k_pos[None, :] < qp[:, None]
        log_keep = jnp.where(mask, jax.nn.log_sigmoid(-z), 0.0)
        later = lax.cumsum(log_keep, axis=z.ndim - 1, reverse=True) - log_keep
        a = jnp.where(mask, jnp.exp(jax.nn.log_sigmoid(z) + later), 0.0).astype(v.dtype)
        return jnp.einsum('bgrts,bsgd->btgrd', a, v)

    o = sweep_q_blocks(block, (qg,), q_pos)
    return o.reshape(B, T, SB_HEADS * HEAD_DIM) @ w_o


def rope_rows(x, pos, rot_dim):
    half = rot_dim // 2
    inv_freq = jnp.power(ROPE_THETA, -jnp.arange(half, dtype=jnp.float32) * 2.0 / rot_dim)
    ang = pos.astype(jnp.float32)[:, None] * inv_freq[None, :]
    cos = jnp.cos(ang)[:, None, :]
    sin = jnp.sin(ang)[:, None, :]
    x1, x2 = x[..., :half], x[..., half:rot_dim]
    return jnp.concatenate([x1 * cos - x2 * sin, x2 * cos + x1 * sin, x[..., rot_dim:]], axis=-1)


def kernel(x_prompt, x_sample, cache_a_latent, cache_b_kv_w128, cache_b_kv_w512, cache_b_kv_w2048,
           cache_c_kv, cache_c_logf, cache_d_kv, page_table,
           ffn_norm_g, ffn_w_gu, ffn_w_down, mix_norm_g,
           a_w_down, a_q_norm_g, a_kv_norm_g, a_w_uq, a_w_uk, a_w_uv, a_w_o,
           b_w_in, b_w_o, c_w_in, c_b_f, c_w_o, d_w_in, d_w_o, final_norm_g):
    depth = ffn_norm_g.shape[0]
    bp, S, _ = x_prompt.shape
    bs, T, _ = x_sample.shape
    assert T == 1
    past = page_table.shape[1] * PAGE_SIZE
    b_caches = (cache_b_kv_w128, cache_b_kv_w512, cache_b_kv_w2048)
    xs2 = [x_prompt.reshape(bp * S, D_MODEL), x_sample.reshape(bs * T, D_MODEL)]
    pos = [jnp.tile(jnp.arange(S, dtype=jnp.int32), bp), jnp.full((bs * T,), past, jnp.int32)]
    lead = [(bp, S), (bs, T)]

    a_out, b_out, ckv_out, clf_out, dkv_out = [[], []], [[[], []] for _ in range(DIL_GROUPS)], [[], []], [[], []], [[], []]

    for i in range(depth):
        m, j = i % N_MIXERS, i // N_MIXERS
        xs2 = [ffn_half(x, ffn_norm_g[i, 0], ffn_w_gu[i, 0], ffn_w_down[i, 0]) for x in xs2]
        mixed = []
        for grp, x in enumerate(xs2):
            nb, nt = lead[grp]
            rows = nb * nt
            if m == 0:
                d = linear(x, a_w_down[j], norm_g=mix_norm_g[i])
                c_q = rms_norm(d[:, :MLA_Q_LORA], a_q_norm_g[j])
                c_kv = rms_norm(d[:, MLA_Q_LORA:MLA_Q_LORA + MLA_KV_LORA], a_kv_norm_g[j])
                k_rope = rope_rows(d[:, None, MLA_Q_LORA + MLA_KV_LORA:], pos[grp], MLA_ROPE)[:, 0]
                latent = jnp.concatenate([c_kv, k_rope], axis=-1)
                q = linear(c_q, a_w_uq[j]).reshape(rows, MLA_HEADS, MLA_NOPE + MLA_ROPE)
                q_rope = rope_rows(q[..., MLA_NOPE:], pos[grp], MLA_ROPE)
                q_lat = headwise_matmul(q[..., :MLA_NOPE].reshape(rows, MLA_HEADS * MLA_NOPE),
                                        jnp.transpose(a_w_uk[j], (1, 2, 0)))
                if grp == 0:
                    o_lat = mla_attention_prompt(q_lat.reshape(nb, nt, -1), q_rope.reshape(nb, nt, -1),
                                                 latent.reshape(nb, nt, -1)).reshape(rows, -1)
                else:
                    q_all = jnp.concatenate([q_lat.reshape(rows, MLA_HEADS, MLA_KV_LORA), q_rope], axis=-1)
                    k_new = jnp.broadcast_to(latent[:, None, :], (rows, MLA_HEADS, MLA_KV_LORA + MLA_ROPE))
                    o_lat = paged_decode("mla", q_all, k_new, k_new[..., :MLA_KV_LORA], cache_a_latent,
                                         page_table, j).reshape(rows, -1)
                o = headwise_matmul(o_lat, jnp.transpose(a_w_uv[j], (1, 0, 2)))
                mixed.append(linear(o, a_w_o[j], residual=x))
                a_out[grp].append(latent.reshape(nb, nt, -1))
            elif m == 1:
                n_dil = DIL_GROUPS * DIL_HEADS * HEAD_DIM
                p = linear(x, b_w_in[j], norm_g=mix_norm_g[i])
                q = rope_rows(p[:, :n_dil].reshape(rows, -1, HEAD_DIM), pos[grp], ROT_DIM)
                k = rope_rows(p[:, n_dil:2 * n_dil].reshape(rows, -1, HEAD_DIM), pos[grp], ROT_DIM)
                v = p[:, 2 * n_dil:].reshape(rows, -1, HEAD_DIM)
                kv = jnp.stack([k, v], axis=1).reshape(nb, nt, 2, DIL_GROUPS, DIL_HEADS, HEAD_DIM)
                if grp == 0:
                    o = dilated_attention_prompt(q.reshape(nb, nt, n_dil), k.reshape(nb, nt, n_dil),
                                                 p.reshape(nb, nt, 3 * n_dil)).reshape(rows, -1)
                else:
                    o = dilated_attention_sample(q.reshape(rows, DIL_GROUPS, DIL_HEADS, HEAD_DIM), kv[:, 0],
                                                 b_caches, j).reshape(rows, -1)
                mixed.append(linear(o, b_w_o[j], residual=x))
                for g, (win, _) in enumerate(DIL_PATTERNS):
                    b_out[g][grp].append(kv[:, nt - min(win, nt):, :, g])
            elif m == 2:
                nq, nkv = FOX_HEADS * HEAD_DIM, FOX_KV_HEADS * HEAD_DIM
                rep = FOX_HEADS // FOX_KV_HEADS
                p = linear(x, c_w_in[j], norm_g=mix_norm_g[i])
                kv = p[:, nq:nq + 2 * nkv].reshape(nb, nt, 2, FOX_KV_HEADS, HEAD_DIM)
                logf = jax.nn.log_sigmoid(p[:, nq + 2 * nkv:] + c_b_f[j]).reshape(nb, nt, FOX_HEADS)
                if grp == 0:
                    o = fox_attention_prompt(p.reshape(nb, nt, -1), jnp.cumsum(logf, axis=1)).reshape(rows, -1)
                else:
                    o = paged_decode("fox", p[:, :nq].reshape(rows, FOX_HEADS, HEAD_DIM),
                                     jnp.repeat(kv[:, 0, 0], rep, axis=1), jnp.repeat(kv[:, 0, 1], rep, axis=1),
                                     cache_c_kv, page_table, j, logf.reshape(rows, FOX_HEADS, 1),
                                     jnp.swapaxes(cache_c_logf, 2, 3)).reshape(rows, -1)
                mixed.append(linear(o, c_w_o[j], residual=x))
                ckv_out[grp].append(kv)
                clf_out[grp].append(logf)
            else:
                nq, nkv = SB_HEADS * HEAD_DIM, SB_KV_HEADS * HEAD_DIM
                rep = SB_HEADS // SB_KV_HEADS
                p = linear(x, d_w_in[j], norm_g=mix_norm_g[i])
                kv = p[:, nq:nq + 2 * nkv].reshape(nb, nt, 2, SB_KV_HEADS, HEAD_DIM)
                if grp == 0:
                    o = sb_attention_prompt(p.reshape(nb, nt, -1)).reshape(rows, -1)
                else:
                    o = paged_decode("sb", p[:, :nq].reshape(rows, SB_HEADS, HEAD_DIM),
                                     jnp.repeat(kv[:, 0, 0], rep, axis=1), jnp.repeat(kv[:, 0, 1], rep, axis=1),
                                     cache_d_kv, page_table, j).reshape(rows, -1)
                mixed.append(linear(o, d_w_o[j], residual=x))
                dkv_out[grp].append(kv)
        xs2 = [ffn_half(x, ffn_norm_g[i, 1], ffn_w_gu[i, 1], ffn_w_down[i, 1]) for x in mixed]

    y_prompt = rms_norm(xs2[0], final_norm_g).reshape(bp, S, D_MODEL)
    y_sample = rms_norm(xs2[1], final_norm_g).reshape(bs, T, D_MODEL)
    return (y_prompt, y_sample, jnp.stack(a_out[0]), jnp.stack(a_out[1]),
            jnp.stack(b_out[0][0]), jnp.stack(b_out[0][1]), jnp.stack(b_out[1][0]), jnp.stack(b_out[1][1]),
            jnp.stack(b_out[2][0]), jnp.stack(b_out[2][1]),
            jnp.stack(ckv_out[0]), jnp.stack(ckv_out[1]), jnp.stack(clf_out[0]), jnp.stack(clf_out[1]),
            jnp.stack(dkv_out[0]), jnp.stack(dkv_out[1]))


def _old_kernel_unused(x_prompt, x_sample, cache_a_latent, cache_b_kv_w128, cache_b_kv_w512, cache_b_kv_w2048,
           cache_c_kv, cache_c_logf, cache_d_kv, page_table,
           ffn_norm_g, ffn_w_gu, ffn_w_down, mix_norm_g,
           a_w_down, a_q_norm_g, a_kv_norm_g, a_w_uq, a_w_uk, a_w_uv, a_w_o,
           b_w_in, b_w_o, c_w_in, c_b_f, c_w_o, d_w_in, d_w_o, final_norm_g):
    DEPTH = ffn_norm_g.shape[0]
    S = x_prompt.shape[1]
    T = x_sample.shape[1]
    P = page_table.shape[1] * PAGE_SIZE
    pos_p = jnp.arange(S, dtype=jnp.int32)
    pos_s = P + jnp.arange(T, dtype=jnp.int32)
    kpos_s = jnp.arange(P + T, dtype=jnp.int32)
    b_caches = (cache_b_kv_w128, cache_b_kv_w512, cache_b_kv_w2048)

    a_p, a_s = [], []
    b_p = [[] for _ in range(DIL_GROUPS)]
    b_s = [[] for _ in range(DIL_GROUPS)]
    ckv_p, ckv_s, clf_p, clf_s = [], [], [], []
    dkv_p, dkv_s = [], []

    xp, xs = x_prompt, x_sample
    for i in range(DEPTH):
        m, j = i % N_MIXERS, i // N_MIXERS
        xp = macaron_half(xp, ffn_norm_g[i, 0], ffn_w_gu[i, 0], ffn_w_down[i, 0])
        xs = macaron_half(xs, ffn_norm_g[i, 0], ffn_w_gu[i, 0], ffn_w_down[i, 0])
        hp = rms_norm(xp, mix_norm_g[i])
        hs = rms_norm(xs, mix_norm_g[i])
        if m == 0:
            qn_p, qr_p, lat_p = mla_project(hp, pos_p, a_w_down[j], a_q_norm_g[j], a_kv_norm_g[j], a_w_uq[j])
            qn_s, qr_s, lat_s = mla_project(hs, pos_s, a_w_down[j], a_q_norm_g[j], a_kv_norm_g[j], a_w_uq[j])
            bp, bs = xp.shape[0], xs.shape[0]
            q_lat_p = jnp.einsum('bthn,chn->bthc', qn_p, a_w_uk[j]).reshape(bp, S, MLA_HEADS * MLA_KV_LORA)
            o_lat_p = mla_attention_prompt(q_lat_p, qr_p.reshape(bp, S, MLA_HEADS * MLA_ROPE), lat_p)
            o_p = jnp.einsum('bthc,chd->bthd', o_lat_p.reshape(bp, S, MLA_HEADS, MLA_KV_LORA), a_w_uv[j])
            mp = o_p.reshape(bp, S, MLA_HEADS * MLA_V) @ a_w_o[j]
            q_lat_s = jnp.einsum('bthn,chn->bthc', qn_s, a_w_uk[j])
            q_all_s = jnp.concatenate([q_lat_s, qr_s], axis=-1)[:, 0]
            k_new = jnp.broadcast_to(lat_s, (bs, MLA_HEADS, MLA_KV_LORA + MLA_ROPE))
            o_lat_s = paged_decode("mla", q_all_s, k_new, k_new[..., :MLA_KV_LORA], cache_a_latent, page_table, j)
            o_s = jnp.einsum('bhc,chd->bhd', o_lat_s, a_w_uv[j])
            ms = o_s.reshape(bs, T, MLA_HEADS * MLA_V) @ a_w_o[j]
            a_p.append(lat_p)
            a_s.append(lat_s)
        elif m == 1:
            q_p, kv_p = dilated_project(hp, pos_p, b_w_in[j])
            q_s, kv_s = dilated_project(hs, pos_s, b_w_in[j])
            for g, (win, dil) in enumerate(DIL_PATTERNS):
                b_p[g].append(kv_p[:, S - min(win, S):, :, g])
                b_s[g].append(kv_s[:, :, :, g])
            n_dil = DIL_GROUPS * DIL_HEADS * HEAD_DIM
            proj_p = hp @ b_w_in[j]
            o_p = dilated_attention_prompt(q_p.reshape(xp.shape[0], S, n_dil),
                                           kv_p[:, :, 0].reshape(xp.shape[0], S, n_dil), proj_p)
            mp = o_p @ b_w_o[j]
            o_s = dilated_attention_sample(q_s[:, 0], kv_s[:, 0], b_caches, j)
            ms = o_s.reshape(xs.shape[0], T, DIL_HEADS * HEAD_DIM) @ b_w_o[j]
        elif m == 2:
            nq_c, nkv_c = FOX_HEADS * HEAD_DIM, FOX_KV_HEADS * HEAD_DIM
            proj_p = hp @ c_w_in[j]
            kv_p = proj_p[..., nq_c:nq_c + 2 * nkv_c].reshape(xp.shape[0], S, 2, FOX_KV_HEADS, HEAD_DIM)
            lf_p = jax.nn.log_sigmoid((proj_p[..., nq_c + 2 * nkv_c:] + c_b_f[j]).astype(jnp.float32))
            q_s, kv_s, lf_s = fox_project(hs, c_w_in[j], c_b_f[j])
            mp = fox_attention_prompt(proj_p, jnp.cumsum(lf_p, axis=1)) @ c_w_o[j]
            rep_c = FOX_HEADS // FOX_KV_HEADS
            o_s = paged_decode("fox", q_s[:, 0], jnp.repeat(kv_s[:, 0, 0], rep_c, axis=1),
                               jnp.repeat(kv_s[:, 0, 1], rep_c, axis=1), cache_c_kv, page_table, j,
                               lf_s.reshape(xs.shape[0], FOX_HEADS, 1), jnp.swapaxes(cache_c_logf, 2, 3))
            ms = o_s.reshape(xs.shape[0], T, FOX_HEADS * HEAD_DIM) @ c_w_o[j]
            ckv_p.append(kv_p)
            ckv_s.append(kv_s)
            clf_p.append(lf_p)
            clf_s.append(lf_s)
        else:
            nq_d, nkv_d = SB_HEADS * HEAD_DIM, SB_KV_HEADS * HEAD_DIM
            proj_p = hp @ d_w_in[j]
            kv_p = proj_p[..., nq_d:nq_d + 2 * nkv_d].reshape(xp.shape[0], S, 2, SB_KV_HEADS, HEAD_DIM)
            q_s, kv_s = sb_project(hs, d_w_in[j])
            mp = sb_attention_prompt(proj_p) @ d_w_o[j]
            rep_d = SB_HEADS // SB_KV_HEADS
            o_s = paged_decode("sb", q_s[:, 0], jnp.repeat(kv_s[:, 0, 0], rep_d, axis=1),
                               jnp.repeat(kv_s[:, 0, 1], rep_d, axis=1), cache_d_kv, page_table, j)
            ms = o_s.reshape(xs.shape[0], T, SB_HEADS * HEAD_DIM) @ d_w_o[j]
            dkv_p.append(kv_p)
            dkv_s.append(kv_s)
        xp = xp + mp.astype(xp.dtype)
        xs = xs + ms.astype(xs.dtype)
        xp = macaron_half(xp, ffn_norm_g[i, 1], ffn_w_gu[i, 1], ffn_w_down[i, 1])
        xs = macaron_half(xs, ffn_norm_g[i, 1], ffn_w_gu[i, 1], ffn_w_down[i, 1])

    y_prompt = rms_norm(xp, final_norm_g)
    y_sample = rms_norm(xs, final_norm_g)
    return (y_prompt, y_sample, jnp.stack(a_p), jnp.stack(a_s),
            jnp.stack(b_p[0]), jnp.stack(b_s[0]), jnp.stack(b_p[1]), jnp.stack(b_s[1]),
            jnp.stack(b_p[2]), jnp.stack(b_s[2]),
            jnp.stack(ckv_p), jnp.stack(ckv_s), jnp.stack(clf_p), jnp.stack(clf_s),
            jnp.stack(dkv_p), jnp.stack(dkv_s))
```

```python
import functools

import jax
import jax.numpy as jnp
from jax import lax
from jax.experimental import pallas as pl
from jax.experimental.pallas import tpu as pltpu

D_MODEL = 1024
PAGE_SIZE = 128
N_MIXERS = 4
HEAD_DIM = 128
N_HEADS = D_MODEL // HEAD_DIM
ROPE_THETA = 500000.0
ROT_DIM = HEAD_DIM // 4
Q_BLOCK = 128
RMS_EPS = 1e-6
D_FF = 2816

MLA_HEADS = N_HEADS
MLA_NOPE = 128
MLA_ROPE = 64
MLA_V = 128
MLA_Q_LORA = 384
MLA_KV_LORA = 256
DIL_PATTERNS = ((128, 1), (512, 4), (2048, 16))
DIL_GROUPS = 3
DIL_HEADS = N_HEADS
FOX_HEADS = N_HEADS
FOX_KV_HEADS = 2
SB_HEADS = N_HEADS
SB_KV_HEADS = 2

V7X_VMEM_LIMIT_BYTES = 56 * 1024 * 1024


def _ffn_kernel(x_ref, g_ref, wgu_ref, wd_ref, o_ref):
    x = x_ref[...]
    ms = jnp.mean(x * x, axis=-1, keepdims=True)
    h = (x * lax.rsqrt(ms + RMS_EPS) * g_ref[...]).astype(jnp.bfloat16)
    gu = jnp.dot(h, wgu_ref[...], preferred_element_type=jnp.float32)
    gate = gu[:, :D_FF]
    up = gu[:, D_FF:]
    act = (gate / (1.0 + jnp.exp(-gate)) * up).astype(jnp.bfloat16)
    o_ref[...] = x + 0.5 * jnp.dot(act, wd_ref[...], preferred_element_type=jnp.float32)


def ffn_half(x2d, g, w_gu, w_down):
    m = x2d.shape[0]
    tm = min(m, 512)
    assert m % tm == 0
    resident = pl.Buffered(1)
    return pl.pallas_call(
        _ffn_kernel,
        grid=(m // tm,),
        in_specs=[
            pl.BlockSpec((tm, D_MODEL), lambda i: (i, 0)),
            pl.BlockSpec((1, D_MODEL), lambda i: (0, 0), pipeline_mode=resident),
            pl.BlockSpec((D_MODEL, 2 * D_FF), lambda i: (0, 0), pipeline_mode=resident),
            pl.BlockSpec((D_FF, D_MODEL), lambda i: (0, 0), pipeline_mode=resident),
        ],
        out_specs=pl.BlockSpec((tm, D_MODEL), lambda i: (i, 0)),
        out_shape=jax.ShapeDtypeStruct((m, D_MODEL), jnp.float32),
        compiler_params=pltpu.CompilerParams(
            dimension_semantics=("arbitrary",), vmem_limit_bytes=V7X_VMEM_LIMIT_BYTES),
        name="ffn_half",
    )(x2d, g.reshape(1, D_MODEL), w_gu.astype(jnp.bfloat16), w_down.astype(jnp.bfloat16))


def macaron_half(x, g, w_gu, w_down):
    b, t, d = x.shape
    return ffn_half(x.reshape(b * t, d), g, w_gu, w_down).reshape(b, t, d)


def _linear_kernel(*refs, has_norm, has_res):
    refs = list(refs)
    x_ref = refs.pop(0)
    g_ref = refs.pop(0) if has_norm else None
    w_ref = refs.pop(0)
    r_ref = refs.pop(0) if has_res else None
    o_ref = refs.pop(0)
    x = x_ref[...]
    if has_norm:
        ms = jnp.mean(x * x, axis=-1, keepdims=True)
        x = x * lax.rsqrt(ms + RMS_EPS) * g_ref[...]
    y = jnp.dot(x.astype(jnp.bfloat16), w_ref[...], preferred_element_type=jnp.float32)
    if has_res:
        y = r_ref[...] + y
    o_ref[...] = y


def _pick_tile(n, cap):
    if n <= cap:
        return n
    best = n
    for t in range(128, cap + 1, 128):
        if n % t == 0:
            best = t
    return best


def linear(x2d, w, norm_g=None, residual=None):
    m, kdim = x2d.shape
    n = w.shape[1]
    tm = _pick_tile(m, 512)
    tn = _pick_tile(n, 2304)
    assert m % tm == 0 and n % tn == 0
    in_specs = [pl.BlockSpec((tm, kdim), lambda jn, im: (im, 0))]
    args = [x2d]
    if norm_g is not None:
        in_specs.append(pl.BlockSpec((1, kdim), lambda jn, im: (0, 0)))
        args.append(norm_g.reshape(1, kdim))
    in_specs.append(pl.BlockSpec((kdim, tn), lambda jn, im: (0, jn)))
    args.append(w.astype(jnp.bfloat16))
    if residual is not None:
        in_specs.append(pl.BlockSpec((tm, tn), lambda jn, im: (im, jn)))
        args.append(residual)
    return pl.pallas_call(
        functools.partial(_linear_kernel, has_norm=norm_g is not None, has_res=residual is not None),
        grid=(n // tn, m // tm),
        in_specs=in_specs,
        out_specs=pl.BlockSpec((tm, tn), lambda jn, im: (im, jn)),
        out_shape=jax.ShapeDtypeStruct((m, n), jnp.float32),
        compiler_params=pltpu.CompilerParams(
            dimension_semantics=("arbitrary", "arbitrary"), vmem_limit_bytes=V7X_VMEM_LIMIT_BYTES),
        name="linear",
    )(*args)


def _linear_rope_kernel(x_ref, g_ref, w_ref, cos_ref, sin_lo_ref, sin_hi_ref, o_ref, *, rope_tiles, shift):
    jn = pl.program_id(0)
    x = x_ref[...]
    ms = jnp.mean(x * x, axis=-1, keepdims=True)
    x = x * lax.rsqrt(ms + RMS_EPS) * g_ref[...]
    y = jnp.dot(x.astype(jnp.bfloat16), w_ref[...], preferred_element_type=jnp.float32)

    @pl.when(jn >= rope_tiles)
    def _():
        o_ref[...] = y

    @pl.when(jn < rope_tiles)
    def _():
        cos, sin_lo, sin_hi = cos_ref[...], sin_lo_ref[...], sin_hi_ref[...]
        for h in range(y.shape[1] // HEAD_DIM):
            yh = y[:, h * HEAD_DIM:(h + 1) * HEAD_DIM]
            ahead = pltpu.roll(yh, HEAD_DIM - shift, 1)
            behind = pltpu.roll(yh, shift, 1)
            o_ref[:, h * HEAD_DIM:(h + 1) * HEAD_DIM] = yh * cos + ahead * sin_lo + behind * sin_hi


def rope_tables(pos, rot_dim):
    half = rot_dim // 2
    inv_freq = jnp.power(ROPE_THETA, -jnp.arange(half, dtype=jnp.float32) * 2.0 / rot_dim)
    ang = pos.astype(jnp.float32)[:, None] * inv_freq[None, :]
    cos, sin = jnp.cos(ang), jnp.sin(ang)
    r = pos.shape[0]
    zeros = lambda w: jnp.zeros((r, w), jnp.float32)
    cos_t = jnp.concatenate([cos, cos, jnp.ones((r, HEAD_DIM - rot_dim), jnp.float32)], axis=1)
    sin_lo = jnp.concatenate([-sin, zeros(HEAD_DIM - half)], axis=1)
    sin_hi = jnp.concatenate([zeros(half), sin, zeros(HEAD_DIM - rot_dim)], axis=1)
    return cos_t, sin_lo, sin_hi


def linear_rope(x2d, w, norm_g, tables, rope_cols, tn):
    m, kdim = x2d.shape
    n = w.shape[1]
    tm = _pick_tile(m, 512)
    period = tables[0].shape[0]
    assert m % tm == 0 and n % tn == 0 and rope_cols % tn == 0 and period % tm == 0
    table_spec = pl.BlockSpec((tm, HEAD_DIM), lambda jn, im: (im % (period // tm), 0))
    return pl.pallas_call(
        functools.partial(_linear_rope_kernel, rope_tiles=rope_cols // tn, shift=ROT_DIM // 2),
        grid=(n // tn, m // tm),
        in_specs=[pl.BlockSpec((tm, kdim), lambda jn, im: (im, 0)),
                  pl.BlockSpec((1, kdim), lambda jn, im: (0, 0)),
                  pl.BlockSpec((kdim, tn), lambda jn, im: (0, jn)),
                  table_spec, table_spec, table_spec],
        out_specs=pl.BlockSpec((tm, tn), lambda jn, im: (im, jn)),
        out_shape=jax.ShapeDtypeStruct((m, n), jnp.float32),
        compiler_params=pltpu.CompilerParams(
            dimension_semantics=("arbitrary", "arbitrary"), vmem_limit_bytes=V7X_VMEM_LIMIT_BYTES),
        name="linear_rope",
    )(x2d, norm_g.reshape(1, kdim), w.astype(jnp.bfloat16), *tables)


def _headwise_kernel(x_ref, w_ref, o_ref, *, n_heads, din, dout):
    for h in range(n_heads):
        xh = x_ref[:, h * din:(h + 1) * din].astype(jnp.bfloat16)
        o_ref[:, h * dout:(h + 1) * dout] = jnp.dot(xh, w_ref[h], preferred_element_type=jnp.float32)


def headwise_matmul(x2d, w):
    m = x2d.shape[0]
    n_heads, din, dout = w.shape
    tm = _pick_tile(m, 512)
    return pl.pallas_call(
        functools.partial(_headwise_kernel, n_heads=n_heads, din=din, dout=dout),
        grid=(m // tm,),
        in_specs=[pl.BlockSpec((tm, n_heads * din), lambda i: (i, 0)),
                  pl.BlockSpec((n_heads, din, dout), lambda i: (0, 0, 0))],
        out_specs=pl.BlockSpec((tm, n_heads * dout), lambda i: (i, 0)),
        out_shape=jax.ShapeDtypeStruct((m, n_heads * dout), jnp.float32),
        compiler_params=pltpu.CompilerParams(
            dimension_semantics=("arbitrary",), vmem_limit_bytes=V7X_VMEM_LIMIT_BYTES),
        name="headwise_matmul",
    )(x2d, w.astype(jnp.bfloat16))


ATTN_BLOCK = 512


def _causal_mask(tq, tk, strict):
    row = lax.broadcasted_iota(jnp.int32, (tq, tk), 0)
    col = lax.broadcasted_iota(jnp.int32, (tq, tk), 1)
    return (col < row) if strict else (col <= row)


def _dot_nt(a, b):
    return lax.dot_general(a, b, (((1,), (1,)), ((), ())), preferred_element_type=jnp.float32)


def _softmax_heads(score_fn, v_ref, o_ref, n_heads, dv, blk):
    qi = pl.program_id(2)
    mask = _causal_mask(blk, blk, strict=False)

    def diagonal(h):
        s = jnp.where(mask, score_fn(h, qi), -1e30)
        m0 = jnp.max(s, axis=-1, keepdims=True)
        p = jnp.exp(s - m0)
        l0 = jnp.sum(p, axis=-1, keepdims=True)
        vd = v_ref[pl.ds(pl.multiple_of(qi * blk, blk), blk), :]
        return m0, l0, jnp.dot(p.astype(jnp.bfloat16), vd, preferred_element_type=jnp.float32)

    def past_block(h, kb, carry):
        m, l, acc = carry
        s = score_fn(h, kb)
        m_new = jnp.maximum(m, jnp.max(s, axis=-1, keepdims=True))
        alpha = jnp.exp(m - m_new)
        p = jnp.exp(s - m_new)
        l = alpha * l + jnp.sum(p, axis=-1, keepdims=True)
        vb = v_ref[pl.ds(pl.multiple_of(kb * blk, blk), blk), :]
        acc = alpha * acc + jnp.dot(p.astype(jnp.bfloat16), vb, preferred_element_type=jnp.float32)
        return m_new, l, acc

    def body(j, carry):
        return tuple(past_block(h, qi - 1 - j, carry[h]) for h in range(n_heads))

    final = lax.fori_loop(0, qi, body, tuple(diagonal(h) for h in range(n_heads)))
    for h, (m, l, acc) in enumerate(final):
        o_ref[:, h * dv:(h + 1) * dv] = acc / l


def _fox_prompt_kernel(q_ref, k_ref, v_ref, cq_ref, ck_ref, o_ref, kb_ref, vb_ref, *, rep, blk):
    qi = pl.program_id(2)

    @pl.when(qi == 0)
    def _():
        kb_ref[...] = k_ref[...].astype(jnp.bfloat16)
        vb_ref[...] = v_ref[...].astype(jnp.bfloat16)

    scale = HEAD_DIM ** -0.5
    qs = [(q_ref[:, h * HEAD_DIM:(h + 1) * HEAD_DIM] * scale).astype(jnp.bfloat16) for h in range(rep)]

    def score(h, kb):
        start = pl.multiple_of(kb * blk, blk)
        kblk = kb_ref[pl.ds(start, blk), :]
        cq = cq_ref[:, h:h + 1]
        ck = ck_ref[pl.ds(h, 1), pl.ds(start, blk)]
        return _dot_nt(qs[h], kblk) + cq - ck

    _softmax_heads(score, vb_ref, o_ref, rep, HEAD_DIM, blk)


def fox_attention_prompt(p, c):
    b, t, _ = p.shape
    blk = min(ATTN_BLOCK, t)
    rep = FOX_HEADS // FOX_KV_HEADS
    nqb = FOX_HEADS
    ck = jnp.transpose(c, (0, 2, 1)).reshape(b, FOX_KV_HEADS, rep, t)
    cq = jnp.transpose(c.reshape(b, t, FOX_KV_HEADS, rep), (0, 2, 1, 3))
    return pl.pallas_call(
        functools.partial(_fox_prompt_kernel, rep=rep, blk=blk),
        grid=(b, FOX_KV_HEADS, t // blk),
        in_specs=[
            pl.BlockSpec((None, blk, rep * HEAD_DIM), lambda i, g, q: (i, q, g)),
            pl.BlockSpec((None, t, HEAD_DIM), lambda i, g, q: (i, 0, nqb + g)),
            pl.BlockSpec((None, t, HEAD_DIM), lambda i, g, q: (i, 0, nqb + FOX_KV_HEADS + g)),
            pl.BlockSpec((None, None, blk, rep), lambda i, g, q: (i, g, q, 0)),
            pl.BlockSpec((None, None, rep, t), lambda i, g, q: (i, g, 0, 0)),
        ],
        out_specs=pl.BlockSpec((None, blk, rep * HEAD_DIM), lambda i, g, q: (i, q, g)),
        out_shape=jax.ShapeDtypeStruct((b, t, FOX_HEADS * HEAD_DIM), jnp.float32),
        scratch_shapes=[pltpu.VMEM((t, HEAD_DIM), jnp.bfloat16), pltpu.VMEM((t, HEAD_DIM), jnp.bfloat16)],
        compiler_params=pltpu.CompilerParams(
            dimension_semantics=("arbitrary", "arbitrary", "arbitrary"), vmem_limit_bytes=V7X_VMEM_LIMIT_BYTES),
        name="fox_attention_prompt",
    )(p, p, p, cq, ck)


def _mla_prompt_kernel(ql_ref, qr_ref, lat_ref, o_ref, kb_ref, *, blk):
    qi = pl.program_id(2)

    @pl.when(qi == 0)
    def _():
        kb_ref[...] = lat_ref[...].astype(jnp.bfloat16)

    scale = (MLA_NOPE + MLA_ROPE) ** -0.5
    ql = [(ql_ref[:, h * MLA_KV_LORA:(h + 1) * MLA_KV_LORA] * scale).astype(jnp.bfloat16) for h in range(MLA_HEADS)]
    qr = [(qr_ref[:, h * MLA_ROPE:(h + 1) * MLA_ROPE] * scale).astype(jnp.bfloat16) for h in range(MLA_HEADS)]

    def score(h, kb):
        start = pl.multiple_of(kb * blk, blk)
        k_lat = kb_ref[pl.ds(start, blk), :MLA_KV_LORA]
        k_rope = kb_ref[pl.ds(start, blk), MLA_KV_LORA:]
        return _dot_nt(ql[h], k_lat) + _dot_nt(qr[h], k_rope)

    _softmax_heads(score, kb_ref.at[:, :MLA_KV_LORA], o_ref, MLA_HEADS, MLA_KV_LORA, blk)


def mla_attention_prompt(q_lat, q_rope, latent):
    b, t, _ = q_lat.shape
    blk = min(ATTN_BLOCK, t)
    dl = MLA_KV_LORA + MLA_ROPE
    return pl.pallas_call(
        functools.partial(_mla_prompt_kernel, blk=blk),
        grid=(b, 1, t // blk),
        in_specs=[
            pl.BlockSpec((None, blk, MLA_HEADS * MLA_KV_LORA), lambda i, g, q: (i, q, 0)),
            pl.BlockSpec((None, blk, MLA_HEADS * MLA_ROPE), lambda i, g, q: (i, q, 0)),
            pl.BlockSpec((None, t, dl), lambda i, g, q: (i, 0, 0)),
        ],
        out_specs=pl.BlockSpec((None, blk, MLA_HEADS * MLA_KV_LORA), lambda i, g, q: (i, q, 0)),
        out_shape=jax.ShapeDtypeStruct((b, t, MLA_HEADS * MLA_KV_LORA), jnp.float32),
        scratch_shapes=[pltpu.VMEM((t, dl), jnp.bfloat16)],
        compiler_params=pltpu.CompilerParams(
            dimension_semantics=("arbitrary", "arbitrary", "arbitrary"), vmem_limit_bytes=V7X_VMEM_LIMIT_BYTES),
        name="mla_attention_prompt",
    )(q_lat, q_rope, latent)


def _sb_prompt_kernel(q_ref, k_ref, v_ref, o_ref, kb_ref, vb_ref, *, rep, blk):
    qi = pl.program_id(2)

    @pl.when(qi == 0)
    def _():
        kb_ref[...] = k_ref[...].astype(jnp.bfloat16)
        vb_ref[...] = v_ref[...].astype(jnp.bfloat16)

    scale = HEAD_DIM ** -0.5
    mask = _causal_mask(blk, blk, strict=True)
    row = lax.broadcasted_iota(jnp.int32, (blk, blk), 0)
    col = lax.broadcasted_iota(jnp.int32, (blk, blk), 1)
    later_sel = jnp.where(row > col, 1.0, 0.0).astype(jnp.bfloat16)

    def block(qh, kb, carry, masked):
        start = pl.multiple_of(kb * blk, blk)
        z = _dot_nt(qh, kb_ref[pl.ds(start, blk), :])
        sp = jnp.log(1.0 + jnp.exp(-jnp.abs(z)))
        log_beta = jnp.minimum(z, 0.0) - sp
        log_keep = jnp.minimum(-z, 0.0) - sp
        if masked:
            log_keep = jnp.where(mask, log_keep, 0.0)
        hi = log_keep.astype(jnp.bfloat16)
        lo = (log_keep - hi.astype(jnp.float32)).astype(jnp.bfloat16)
        later = (jnp.dot(hi, later_sel, preferred_element_type=jnp.float32)
                 + jnp.dot(lo, later_sel, preferred_element_type=jnp.float32))
        if carry is not None:
            later = later + carry[0]
        a = jnp.exp(log_beta + later)
        if masked:
            a = jnp.where(mask, a, 0.0)
        pv = jnp.dot(a.astype(jnp.bfloat16), vb_ref[pl.ds(start, blk), :], preferred_element_type=jnp.float32)
        tail = jnp.sum(log_keep, axis=-1, keepdims=True)
        if carry is None:
            return tail, pv
        return carry[0] + tail, carry[1] + pv

    qs = [(q_ref[:, h * HEAD_DIM:(h + 1) * HEAD_DIM] * scale).astype(jnp.bfloat16) for h in range(rep)]

    def body(j, carry):
        return tuple(block(qs[h], qi - 1 - j, carry[h], False) for h in range(rep))

    final = lax.fori_loop(0, qi, body, tuple(block(qs[h], qi, None, True) for h in range(rep)))
    for h, (_, acc) in enumerate(final):
        o_ref[:, h * HEAD_DIM:(h + 1) * HEAD_DIM] = acc


def sb_attention_prompt(p):
    b, t, _ = p.shape
    blk = min(ATTN_BLOCK, t)
    rep = SB_HEADS // SB_KV_HEADS
    nqb = SB_HEADS
    return pl.pallas_call(
        functools.partial(_sb_prompt_kernel, rep=rep, blk=blk),
        grid=(b, SB_KV_HEADS, t // blk),
        in_specs=[
            pl.BlockSpec((None, blk, rep * HEAD_DIM), lambda i, g, q: (i, q, g)),
            pl.BlockSpec((None, t, HEAD_DIM), lambda i, g, q: (i, 0, nqb + g)),
            pl.BlockSpec((None, t, HEAD_DIM), lambda i, g, q: (i, 0, nqb + SB_KV_HEADS + g)),
        ],
        out_specs=pl.BlockSpec((None, blk, rep * HEAD_DIM), lambda i, g, q: (i, q, g)),
        out_shape=jax.ShapeDtypeStruct((b, t, SB_HEADS * HEAD_DIM), jnp.float32),
        scratch_shapes=[pltpu.VMEM((t, HEAD_DIM), jnp.bfloat16), pltpu.VMEM((t, HEAD_DIM), jnp.bfloat16)],
        compiler_params=pltpu.CompilerParams(
            dimension_semantics=("arbitrary", "arbitrary", "arbitrary"), vmem_limit_bytes=V7X_VMEM_LIMIT_BYTES),
        name="sb_attention_prompt",
    )(p, p, p)


DIL_N = 128
DIL_UNROLL = 4


def _dilated_prompt_kernel(*refs, t):
    qkv = refs[:9]
    o_ref, og_ref, lse_ref = refs[9:]
    scale = HEAD_DIM ** -0.5
    n = DIL_N
    qi = lax.broadcasted_iota(jnp.int32, (n, 2 * n), 0)
    kj = lax.broadcasted_iota(jnp.int32, (n, 2 * n), 1)
    cur_ok = (kj >= n) & (kj - n <= qi)
    prev_ok = (kj < n) & (kj >= qi)

    for g, (_, dil) in enumerate(DIL_PATTERNS):
        q_ref, k_ref, v_ref = qkv[3 * g:3 * g + 3]
        nb = t // (dil * n)

        def unit(u, q_ref=q_ref, k_ref=k_ref, v_ref=v_ref, dil=dil, nb=nb, g=g):
            r, i = u // nb, u % nb
            cur = r + dil * n * i
            prev = r + dil * n * jnp.maximum(i - 1, 0)
            rows = lambda ref, start: ref[pl.ds(start, n, stride=dil), :]
            qb = (rows(q_ref, cur) * scale).astype(jnp.bfloat16)
            kb = jnp.concatenate([rows(k_ref, prev), rows(k_ref, cur)], axis=0).astype(jnp.bfloat16)
            vb = jnp.concatenate([rows(v_ref, prev), rows(v_ref, cur)], axis=0).astype(jnp.bfloat16)
            s = _dot_nt(qb, kb)
            s = jnp.where(cur_ok | (prev_ok & (i > 0)), s, -1e30)
            m = jnp.max(s, axis=-1, keepdims=True)
            p = jnp.exp(s - m)
            l = jnp.sum(p, axis=-1, keepdims=True)
            o = jnp.dot(p.astype(jnp.bfloat16), vb, preferred_element_type=jnp.float32) / l
            og_ref.at[g][pl.ds(cur, n, stride=dil), :] = o
            lse_ref.at[g][pl.ds(cur, n, stride=dil), :] = jnp.broadcast_to(m + jnp.log(l), (n, HEAD_DIM))

        def body(j, carry, unit=unit):
            for uu in range(DIL_UNROLL):
                unit(j * DIL_UNROLL + uu)
            return carry

        lax.fori_loop(0, t // n // DIL_UNROLL, body, 0)

    lses = [lse_ref[g] for g in range(DIL_GROUPS)]
    top = functools.reduce(jnp.maximum, lses)
    ws = [jnp.exp(x - top) for x in lses]
    den = functools.reduce(lambda a, b: a + b, ws)
    o_ref[...] = functools.reduce(lambda a, b: a + b, [w * og_ref[g] for g, w in enumerate(ws)]) / den


def dilated_attention_prompt(p):
    b, t, _ = p.shape
    assert all(t % (dil * DIL_N) == 0 for _, dil in DIL_PATTERNS)
    n_gh = DIL_GROUPS * DIL_HEADS
    in_specs, args = [], []
    for g in range(DIL_GROUPS):
        for part in range(3):
            in_specs.append(pl.BlockSpec((None, t, HEAD_DIM),
                                         lambda i, h, g=g, part=part: (i, 0, part * n_gh + g * DIL_HEADS + h)))
        args += [p, p, p]
    return pl.pallas_call(
        functools.partial(_dilated_prompt_kernel, t=t),
        grid=(b, DIL_HEADS),
        in_specs=in_specs,
        out_specs=pl.BlockSpec((None, t, HEAD_DIM), lambda i, h: (i, 0, h)),
        out_shape=jax.ShapeDtypeStruct((b, t, DIL_HEADS * HEAD_DIM), jnp.float32),
        scratch_shapes=[pltpu.VMEM((DIL_GROUPS, t, HEAD_DIM), jnp.float32),
                        pltpu.VMEM((DIL_GROUPS, t, HEAD_DIM), jnp.float32)],
        compiler_params=pltpu.CompilerParams(
            dimension_semantics=("arbitrary", "arbitrary"), vmem_limit_bytes=V7X_VMEM_LIMIT_BYTES),
        name="dilated_attention_prompt",
    )(*args)


def _dilated_sample_kernel(q_ref, kvn_ref, c0_ref, c1_ref, c2_ref, o_ref):
    scale = HEAD_DIM ** -0.5
    outs, lses = [], []
    for g, c_ref in enumerate((c0_ref, c1_ref, c2_ref)):
        qg = q_ref[g] * scale
        kc, vc = c_ref[:, 0], c_ref[:, 1]
        kn, vn = kvn_ref[0, g], kvn_ref[1, g]
        s_c = jnp.sum(kc * qg[None], axis=-1, keepdims=True)
        s_n = jnp.sum(kn * qg, axis=-1, keepdims=True)
        m = jnp.maximum(jnp.max(s_c, axis=0), s_n)
        p_c = jnp.exp(s_c - m[None])
        p_n = jnp.exp(s_n - m)
        l = jnp.sum(p_c, axis=0) + p_n
        outs.append((jnp.sum(p_c * vc, axis=0) + p_n * vn) / l)
        lses.append(m + jnp.log(l))
    top = functools.reduce(jnp.maximum, lses)
    ws = [jnp.exp(x - top) for x in lses]
    den = functools.reduce(lambda a, b: a + b, ws)
    o_ref[...] = functools.reduce(lambda a, b: a + b, [w * o for w, o in zip(ws, outs)]) / den


def dilated_attention_sample(q, kv_new, caches, layer):
    bd = q.shape[0]
    in_specs = [pl.BlockSpec((None, DIL_GROUPS, DIL_HEADS, HEAD_DIM), lambda i: (i, 0, 0, 0)),
                pl.BlockSpec((None, 2, DIL_GROUPS, DIL_HEADS, HEAD_DIM), lambda i: (i, 0, 0, 0, 0))]
    args = [q, kv_new]
    for (win, dil), cache in zip(DIL_PATTERNS, caches):
        assert cache.shape[2] == win == DIL_N * dil
        args.append(cache.reshape(cache.shape[0], bd, DIL_N, dil, 2, DIL_HEADS, HEAD_DIM))
        in_specs.append(pl.BlockSpec((None, None, DIL_N, None, 2, DIL_HEADS, HEAD_DIM),
                                     lambda i: (layer, i, 0, 0, 0, 0, 0)))
    return pl.pallas_call(
        _dilated_sample_kernel,
        grid=(bd,),
        in_specs=in_specs,
        out_specs=pl.BlockSpec((None, DIL_HEADS, HEAD_DIM), lambda i: (i, 0, 0)),
        out_shape=jax.ShapeDtypeStruct((bd, DIL_HEADS, HEAD_DIM), jnp.float32),
        compiler_params=pltpu.CompilerParams(
            dimension_semantics=("arbitrary",), vmem_limit_bytes=V7X_VMEM_LIMIT_BYTES),
        name="dilated_attention_sample",
    )(*args)


DECODE_PAGES = 16


def _split3(x):
    hi = x.astype(jnp.bfloat16).astype(jnp.float32)
    r = x - hi
    mid = r.astype(jnp.bfloat16).astype(jnp.float32)
    lo = r - mid
    return hi, mid, lo


def _dot_split(xs, w_bf16):
    n = xs[0].shape[0]
    pieces = [piece for x in xs for piece in _split3(x)]
    r = jnp.dot(jnp.concatenate(pieces, axis=0).astype(jnp.bfloat16), w_bf16, preferred_element_type=jnp.float32)
    return [r[3 * i * n:(3 * i + 1) * n] + r[(3 * i + 1) * n:(3 * i + 2) * n] + r[(3 * i + 2) * n:(3 * i + 3) * n]
            for i in range(len(xs))]


def _decode_kernel(pt_ref, *refs, mode, n_pages_step, scale):
    del pt_ref
    np_ = n_pages_step
    if mode == "fox":
        q_ref, kn_ref, vn_ref, lfn_ref = refs[:4]
        rest = refs[4:]
        page_refs, lf_refs, rest = rest[:np_], rest[np_:2 * np_], rest[2 * np_:]
    else:
        q_ref, kn_ref, vn_ref = refs[:3]
        rest = refs[3:]
        page_refs, rest = rest[:np_], rest[np_:]
        lf_refs = None
    o_ref, m_sc, l_sc, acc_sc, r_sc = rest
    c = pl.program_id(1)
    n_chunks = pl.num_programs(1)
    gqa = mode != "mla"
    q = q_ref[...] * scale
    qb = q.astype(jnp.bfloat16)
    n_heads = q.shape[0]
    row = lax.broadcasted_iota(jnp.int32, (n_heads, PAGE_SIZE), 0)
    first_group = row < (n_heads // 2)

    @pl.when(c == 0)
    def _():
        if mode == "sb":
            acc_sc[...] = jnp.zeros_like(acc_sc)
            r_sc[...] = jnp.zeros_like(r_sc)
        else:
            m_sc[...] = jnp.sum(q * kn_ref[...], axis=-1, keepdims=True)
            l_sc[...] = jnp.ones_like(l_sc)
            acc_sc[...] = jnp.broadcast_to(vn_ref[...], acc_sc.shape)
            if mode == "fox":
                r_sc[...] = lfn_ref[...]

    def keys_values(k):
        if gqa:
            pr = page_refs[k]
            kcat = jnp.concatenate([pr[pl.ds(g, PAGE_SIZE, stride=4), :] for g in range(2)], axis=0)
            vcat = jnp.concatenate([pr[pl.ds(2 + g, PAGE_SIZE, stride=4), :] for g in range(2)], axis=0)
            return kcat.astype(jnp.bfloat16), vcat.astype(jnp.bfloat16)
        kt = page_refs[k][...].astype(jnp.bfloat16)
        return kt, kt[:MLA_KV_LORA]

    def qk(keys):
        if gqa:
            r = _dot_nt(qb, keys)
            return jnp.where(first_group, r[:, :PAGE_SIZE], r[:, PAGE_SIZE:])
        return jnp.dot(qb, keys, preferred_element_type=jnp.float32)

    def pv_dot(w, vals):
        if gqa:
            w = jnp.concatenate([jnp.where(first_group, w, 0.0), jnp.where(first_group, 0.0, w)], axis=1)
            return jnp.dot(w.astype(jnp.bfloat16), vals, preferred_element_type=jnp.float32)
        return _dot_nt(w.astype(jnp.bfloat16), vals)

    if mode in ("fox", "sb"):
        rr = lax.broadcasted_iota(jnp.int32, (PAGE_SIZE, PAGE_SIZE), 0)
        cc = lax.broadcasted_iota(jnp.int32, (PAGE_SIZE, PAGE_SIZE), 1)
        later_sel = jnp.where(rr > cc, 1.0, 0.0).astype(jnp.bfloat16)

    def suffix_terms(per_page):
        within = _dot_split(per_page, later_sel)
        run = r_sc[...]
        out = []
        for k in range(np_):
            out.append(within[k] + run)
            run = run + jnp.sum(per_page[k], axis=-1, keepdims=True)
        r_sc[...] = run
        return out

    kvs = [keys_values(k) for k in range(np_)]
    values = [kv[1] for kv in kvs]
    scores = [qk(kv[0]) for kv in kvs]
    if mode == "fox":
        bias = suffix_terms([lf_refs[k][...] for k in range(np_)])
        scores = [s + b for s, b in zip(scores, bias)]
    if mode == "sb":
        z = jnp.concatenate(scores, axis=1)
        sp = jnp.log(1.0 + jnp.exp(-jnp.abs(z)))
        log_beta = jnp.minimum(z, 0.0) - sp
        log_keep = jnp.minimum(-z, 0.0) - sp
        later = suffix_terms([log_keep[:, k * PAGE_SIZE:(k + 1) * PAGE_SIZE] for k in range(np_)])
        a = jnp.exp(log_beta + jnp.concatenate(later, axis=1))
        pv = None
        for k in range(np_):
            t = pv_dot(a[:, k * PAGE_SIZE:(k + 1) * PAGE_SIZE], values[k])
            pv = t if pv is None else pv + t
        acc_sc[...] = acc_sc[...] + pv
    else:
        s_all = jnp.concatenate(scores, axis=1)
        m_prev = m_sc[...]
        m_new = jnp.maximum(m_prev, jnp.max(s_all, axis=-1, keepdims=True))
        alpha = jnp.exp(m_prev - m_new)
        p_all = jnp.exp(s_all - m_new)
        l_sc[...] = alpha * l_sc[...] + jnp.sum(p_all, axis=-1, keepdims=True)
        pv = None
        for k in range(np_):
            t = pv_dot(p_all[:, k * PAGE_SIZE:(k + 1) * PAGE_SIZE], values[k])
            pv = t if pv is None else pv + t
        acc_sc[...] = alpha * acc_sc[...] + pv
        m_sc[...] = m_new

    @pl.when(c == n_chunks - 1)
    def _():
        if mode == "sb":
            o_ref[...] = acc_sc[...]
        else:
            o_ref[...] = acc_sc[...] / l_sc[...]


def paged_decode(mode, q, k_new, v_new, pool, page_table, layer, lf_new=None, lf_pool_t=None):
    bd, n_heads, dq = q.shape
    dv = v_new.shape[-1]
    n_pages = page_table.shape[1]
    np_ = min(DECODE_PAGES, n_pages)
    assert n_pages % np_ == 0
    scale = (MLA_NOPE + MLA_ROPE) ** -0.5 if mode == "mla" else HEAD_DIM ** -0.5
    if mode == "mla":
        pool = jnp.swapaxes(pool, 2, 3)
    else:
        pool = pool.reshape(pool.shape[0], pool.shape[1], PAGE_SIZE * 4, HEAD_DIM)
    page_block = (None, None) + pool.shape[2:]
    zeros = (0,) * (pool.ndim - 2)

    def page_spec(k, block, tail):
        def imap(b, c, pt):
            return (layer, pt[b, n_pages - 1 - (c * np_ + k)]) + tail
        return pl.BlockSpec(block, imap)

    def head_spec(d):
        return pl.BlockSpec((None, n_heads, d), lambda b, c, pt: (b, 0, 0))

    in_specs = [head_spec(dq), head_spec(dq), head_spec(dv)]
    args = [q, k_new, v_new]
    if mode == "fox":
        in_specs.append(head_spec(1))
        args.append(lf_new)
    in_specs += [page_spec(k, page_block, zeros) for k in range(np_)]
    args += [pool] * np_
    if mode == "fox":
        in_specs += [page_spec(k, (None, None, n_heads, PAGE_SIZE), (0, 0)) for k in range(np_)]
        args += [lf_pool_t] * np_
    return pl.pallas_call(
        functools.partial(_decode_kernel, mode=mode, n_pages_step=np_, scale=scale),
        grid_spec=pltpu.PrefetchScalarGridSpec(
            num_scalar_prefetch=1,
            grid=(bd, n_pages // np_),
            in_specs=in_specs,
            out_specs=pl.BlockSpec((None, n_heads, dv), lambda b, c, pt: (b, 0, 0)),
            scratch_shapes=[pltpu.VMEM((n_heads, 1), jnp.float32), pltpu.VMEM((n_heads, 1), jnp.float32),
                            pltpu.VMEM((n_heads, dv), jnp.float32), pltpu.VMEM((n_heads, 1), jnp.float32)],
        ),
        out_shape=jax.ShapeDtypeStruct((bd, n_heads, dv), jnp.float32),
        compiler_params=pltpu.CompilerParams(
            dimension_semantics=("arbitrary", "arbitrary"), vmem_limit_bytes=V7X_VMEM_LIMIT_BYTES),
        name="paged_decode_" + mode,
    )(page_table, *args)


def rms_norm(x, g):
    xf = x.astype(jnp.float32)
    y = xf * lax.rsqrt(jnp.mean(xf * xf, axis=-1, keepdims=True) + RMS_EPS)
    return (y * g.astype(jnp.float32)).astype(x.dtype)


def apply_rope(x, pos, rot_dim):
    half = rot_dim // 2
    inv_freq = jnp.power(ROPE_THETA, -jnp.arange(half, dtype=jnp.float32) * 2.0 / rot_dim)
    ang = pos.astype(jnp.float32)[:, None] * inv_freq[None, :]
    cos = jnp.cos(ang)[None, :, None, :]
    sin = jnp.sin(ang)[None, :, None, :]
    xr = x[..., :rot_dim].astype(jnp.float32)
    x1, x2 = xr[..., :half], xr[..., half:]
    rot = jnp.concatenate([x1 * cos - x2 * sin, x2 * cos + x1 * sin], axis=-1)
    return jnp.concatenate([rot.astype(x.dtype), x[..., rot_dim:]], axis=-1)


def gather_pages(pool, layer, page_table):
    g = pool[layer, page_table]
    return g.reshape(g.shape[0], g.shape[1] * g.shape[2], *g.shape[3:])


def sweep_q_blocks(fn, qs, q_pos):
    T = q_pos.shape[0]
    if T > Q_BLOCK and T % Q_BLOCK == 0:
        nb = T // Q_BLOCK
        qb = tuple(jnp.swapaxes(a.reshape(a.shape[0], nb, Q_BLOCK, *a.shape[2:]), 0, 1) for a in qs)
        pb = q_pos.reshape(nb, Q_BLOCK)
        out = lax.map(lambda args: fn(*args[0], args[1]), (qb, pb))
        out = jnp.swapaxes(out, 0, 1)
        return out.reshape(out.shape[0], T, *out.shape[3:])
    return fn(*qs, q_pos)


def mla_project(h, pos, w_down, q_norm_g, kv_norm_g, w_uq):
    B, T, _ = h.shape
    d = h @ w_down
    c_q = rms_norm(d[..., :MLA_Q_LORA], q_norm_g)
    c_kv = rms_norm(d[..., MLA_Q_LORA:MLA_Q_LORA + MLA_KV_LORA], kv_norm_g)
    k_rope = apply_rope(d[:, :, None, MLA_Q_LORA + MLA_KV_LORA:], pos, MLA_ROPE)[:, :, 0]
    q = (c_q @ w_uq).reshape(B, T, MLA_HEADS, MLA_NOPE + MLA_ROPE)
    q_nope = q[..., :MLA_NOPE]
    q_rope = apply_rope(q[..., MLA_NOPE:], pos, MLA_ROPE)
    latent = jnp.concatenate([c_kv, k_rope], axis=-1)
    return q_nope, q_rope, latent


def mla_attend(q_nope, q_rope, q_pos, latent, k_pos, w_uk, w_uv, w_o):
    B, T = q_nope.shape[:2]
    c_kv = latent[..., :MLA_KV_LORA]
    k_rope = latent[..., MLA_KV_LORA:]
    q_lat = jnp.einsum('bthn,chn->bthc', q_nope, w_uk)
    scale = (MLA_NOPE + MLA_ROPE) ** -0.5

    def block(ql, qr, qp):
        s = (jnp.einsum('bthc,bsc->bhts', ql, c_kv)
             + jnp.einsum('bthr,bsr->bhts', qr, k_rope)).astype(jnp.float32) * scale
        s = jnp.where(k_pos[None, :] <= qp[:, None], s, -jnp.inf)
        p = jax.nn.softmax(s, axis=-1).astype(c_kv.dtype)
        return jnp.einsum('bhts,bsc->bthc', p, c_kv)

    o_lat = sweep_q_blocks(block, (q_lat, q_rope), q_pos)
    o = jnp.einsum('bthc,chd->bthd', o_lat, w_uv).reshape(B, T, MLA_HEADS * MLA_V)
    return o @ w_o


def dilated_project(h, pos, w_in):
    B, T, _ = h.shape
    qkv = (h @ w_in).reshape(B, T, 3, DIL_GROUPS * DIL_HEADS, HEAD_DIM)
    q = apply_rope(qkv[:, :, 0], pos, ROT_DIM).reshape(B, T, DIL_GROUPS, DIL_HEADS, HEAD_DIM)
    k = apply_rope(qkv[:, :, 1], pos, ROT_DIM).reshape(B, T, DIL_GROUPS, DIL_HEADS, HEAD_DIM)
    v = qkv[:, :, 2].reshape(B, T, DIL_GROUPS, DIL_HEADS, HEAD_DIM)
    return q, jnp.stack([k, v], axis=2)


def dilated_group_prompt(q, kv, dil, n):
    B, S, H, D = q.shape
    L = S // dil
    nb = -(-L // n)
    Lp = nb * n

    def to_sub(a):
        a = a.reshape(B, L, dil, H, D).transpose(0, 2, 1, 3, 4)
        return jnp.pad(a, ((0, 0), (0, 0), (0, Lp - L), (0, 0), (0, 0)))

    def window(a):
        a = jnp.pad(a, ((0, 0), (0, 0), (n, 0), (0, 0), (0, 0))).reshape(B, dil, nb + 1, n, H, D)
        return jnp.concatenate([a[:, :, :-1], a[:, :, 1:]], axis=3)

    qb = to_sub(q).reshape(B, dil, nb, n, H, D)
    kw = window(to_sub(kv[:, :, 0]))
    vw = window(to_sub(kv[:, :, 1]))
    s = jnp.einsum('brnqhd,brnkhd->brnhqk', qb, kw).astype(jnp.float32) * (HEAD_DIM ** -0.5)
    qi = jnp.arange(n)[:, None]
    kj = jnp.arange(2 * n)[None, :]
    dist = qi + n - kj
    key_idx = jnp.arange(nb)[:, None, None] * n - n + kj[None]
    mask = (dist >= 0) & (dist <= n) & (key_idx >= 0)
    s = jnp.where(mask[None, None, :, None], s, -jnp.inf)
    lse = jax.nn.logsumexp(s, axis=-1)
    p = jnp.exp(s - lse[..., None]).astype(q.dtype)
    o = jnp.einsum('brnhqk,brnkhd->brnqhd', p, vw)
    o = o.reshape(B, dil, Lp, H, D)[:, :, :L].transpose(0, 2, 1, 3, 4).reshape(B, S, H, D)
    lse = lse.transpose(0, 1, 2, 4, 3).reshape(B, dil, Lp, H)[:, :, :L].transpose(0, 2, 1, 3).reshape(B, S, H)
    return o, lse


def dilated_group_sample(q, kv_all, dil, n):
    B, T = q.shape[:2]
    L_buf = kv_all.shape[1] - T
    idx = L_buf + jnp.arange(T)[:, None] - dil * jnp.arange(n + 1)[None, :]
    valid = idx >= 0
    g = kv_all[:, jnp.maximum(idx, 0)]
    s = jnp.einsum('bthd,btkhd->bthk', q, g[:, :, :, 0]).astype(jnp.float32) * (HEAD_DIM ** -0.5)
    s = jnp.where(valid[None, :, None, :], s, -jnp.inf)
    lse = jax.nn.logsumexp(s, axis=-1)
    p = jnp.exp(s - lse[..., None]).astype(q.dtype)
    o = jnp.einsum('bthk,btkhd->bthd', p, g[:, :, :, 1])
    return o, lse


def dilated_merge(outs, lses, w_o):
    o = jnp.stack(outs, axis=0)
    wts = jax.nn.softmax(jnp.stack(lses, axis=0), axis=0).astype(o.dtype)
    o = jnp.einsum('gbth,gbthd->bthd', wts, o)
    return o.reshape(o.shape[0], o.shape[1], DIL_HEADS * HEAD_DIM) @ w_o


def fox_project(h, w_in, b_f):
    B, T, _ = h.shape
    nq, nkv = FOX_HEADS * HEAD_DIM, FOX_KV_HEADS * HEAD_DIM
    p = h @ w_in
    q = p[..., :nq].reshape(B, T, FOX_HEADS, HEAD_DIM)
    kv = p[..., nq:nq + 2 * nkv].reshape(B, T, 2, FOX_KV_HEADS, HEAD_DIM)
    logf = jax.nn.log_sigmoid((p[..., nq + 2 * nkv:] + b_f).astype(jnp.float32))
    return q, kv, logf


def fox_attend(q, q_pos, kv_all, logf_all, k_pos, w_o):
    B, T = q.shape[:2]
    S = kv_all.shape[1]
    rep = FOX_HEADS // FOX_KV_HEADS
    c = jnp.cumsum(logf_all.astype(jnp.float32), axis=1)
    c_q = c[:, S - T:]
    c_k = c.reshape(B, S, FOX_KV_HEADS, rep).transpose(0, 2, 3, 1)
    k, v = kv_all[:, :, 0], kv_all[:, :, 1]
    qg = q.reshape(B, T, FOX_KV_HEADS, rep, HEAD_DIM)
    scale = HEAD_DIM ** -0.5

    def block(qb, cqb, qp):
        s = jnp.einsum('btgrd,bsgd->bgrts', qb, k).astype(jnp.float32) * scale
        cq = cqb.reshape(B, -1, FOX_KV_HEADS, rep).transpose(0, 2, 3, 1)[..., None]
        s = s + cq - c_k[:, :, :, None, :]
        s = jnp.where(k_pos[None, :] <= qp[:, None], s, -jnp.inf)
        p = jax.nn.softmax(s, axis=-1).astype(v.dtype)
        return jnp.einsum('bgrts,bsgd->btgrd', p, v)

    o = sweep_q_blocks(block, (qg, c_q), q_pos)
    return o.reshape(B, T, FOX_HEADS * HEAD_DIM) @ w_o


def sb_project(h, w_in):
    B, T, _ = h.shape
    nq, nkv = SB_HEADS * HEAD_DIM, SB_KV_HEADS * HEAD_DIM
    p = h @ w_in
    q = p[..., :nq].reshape(B, T, SB_HEADS, HEAD_DIM)
    kv = p[..., nq:nq + 2 * nkv].reshape(B, T, 2, SB_KV_HEADS, HEAD_DIM)
    return q, kv


def sb_attend(q, q_pos, kv_all, k_pos, w_o):
    B, T = q.shape[:2]
    rep = SB_HEADS // SB_KV_HEADS
    k, v = kv_all[:, :, 0], kv_all[:, :, 1]
    qg = q.reshape(B, T, SB_KV_HEADS, rep, HEAD_DIM)
    scale = HEAD_DIM ** -0.5

    def block(qb, qp):
        z = jnp.einsum('btgrd,bsgd->bgrts', qb, k).astype(jnp.float32) * scale
        mask = k_pos[None, :] < qp[:, None]
        log_keep = jnp.where(mask, jax.nn.log_sigmoid(-z), 0.0)
        later = lax.cumsum(log_keep, axis=z.ndim - 1, reverse=True) - log_keep
        a = jnp.where(mask, jnp.exp(jax.nn.log_sigmoid(z) + later), 0.0).astype(v.dtype)
        return jnp.einsum('bgrts,bsgd->btgrd', a, v)

    o = sweep_q_blocks(block, (qg,), q_pos)
    return o.reshape(B, T, SB_HEADS * HEAD_DIM) @ w_o


def rope_rows(x, pos, rot_dim):
    half = rot_dim // 2
    inv_freq = jnp.power(ROPE_THETA, -jnp.arange(half, dtype=jnp.float32) * 2.0 / rot_dim)
    ang = pos.astype(jnp.float32)[:, None] * inv_freq[None, :]
    cos = jnp.cos(ang)[:, None, :]
    sin = jnp.sin(ang)[:, None, :]
    x1, x2 = x[..., :half], x[..., half:rot_dim]
    return jnp.concatenate([x1 * cos - x2 * sin, x2 * cos + x1 * sin, x[..., rot_dim:]], axis=-1)


def kernel(x_prompt, x_sample, cache_a_latent, cache_b_kv_w128, cache_b_kv_w512, cache_b_kv_w2048,
           cache_c_kv, cache_c_logf, cache_d_kv, page_table,
           ffn_norm_g, ffn_w_gu, ffn_w_down, mix_norm_g,
           a_w_down, a_q_norm_g, a_kv_norm_g, a_w_uq, a_w_uk, a_w_uv, a_w_o,
           b_w_in, b_w_o, c_w_in, c_b_f, c_w_o, d_w_in, d_w_o, final_norm_g):
    depth = ffn_norm_g.shape[0]
    bp, S, _ = x_prompt.shape
    bs, T, _ = x_sample.shape
    assert T == 1
    past = page_table.shape[1] * PAGE_SIZE
    b_caches = (cache_b_kv_w128, cache_b_kv_w512, cache_b_kv_w2048)
    xs2 = [x_prompt.reshape(bp * S, D_MODEL), x_sample.reshape(bs * T, D_MODEL)]
    pos = [jnp.tile(jnp.arange(S, dtype=jnp.int32), bp), jnp.full((bs * T,), past, jnp.int32)]
    lead = [(bp, S), (bs, T)]

    a_out, b_out, ckv_out, clf_out, dkv_out = [[], []], [[[], []] for _ in range(DIL_GROUPS)], [[], []], [[], []], [[], []]

    for i in range(depth):
        m, j = i % N_MIXERS, i // N_MIXERS
        xs2 = [ffn_half(x, ffn_norm_g[i, 0], ffn_w_gu[i, 0], ffn_w_down[i, 0]) for x in xs2]
        mixed = []
        for grp, x in enumerate(xs2):
            nb, nt = lead[grp]
            rows = nb * nt
            if m == 0:
                d = linear(x, a_w_down[j], norm_g=mix_norm_g[i])
                c_q = rms_norm(d[:, :MLA_Q_LORA], a_q_norm_g[j])
                c_kv = rms_norm(d[:, MLA_Q_LORA:MLA_Q_LORA + MLA_KV_LORA], a_kv_norm_g[j])
                k_rope = rope_rows(d[:, None, MLA_Q_LORA + MLA_KV_LORA:], pos[grp], MLA_ROPE)[:, 0]
                latent = jnp.concatenate([c_kv, k_rope], axis=-1)
                q = linear(c_q, a_w_uq[j]).reshape(rows, MLA_HEADS, MLA_NOPE + MLA_ROPE)
                q_rope = rope_rows(q[..., MLA_NOPE:], pos[grp], MLA_ROPE)
                q_lat = headwise_matmul(q[..., :MLA_NOPE].reshape(rows, MLA_HEADS * MLA_NOPE),
                                        jnp.transpose(a_w_uk[j], (1, 2, 0)))
                if grp == 0:
                    o_lat = mla_attention_prompt(q_lat.reshape(nb, nt, -1), q_rope.reshape(nb, nt, -1),
                                                 latent.reshape(nb, nt, -1)).reshape(rows, -1)
                else:
                    q_all = jnp.concatenate([q_lat.reshape(rows, MLA_HEADS, MLA_KV_LORA), q_rope], axis=-1)
                    k_new = jnp.broadcast_to(latent[:, None, :], (rows, MLA_HEADS, MLA_KV_LORA + MLA_ROPE))
                    o_lat = paged_decode("mla", q_all, k_new, k_new[..., :MLA_KV_LORA], cache_a_latent,
                                         page_table, j).reshape(rows, -1)
                o = headwise_matmul(o_lat, jnp.transpose(a_w_uv[j], (1, 0, 2)))
                mixed.append(linear(o, a_w_o[j], residual=x))
                a_out[grp].append(latent.reshape(nb, nt, -1))
            elif m == 1:
                n_dil = DIL_GROUPS * DIL_HEADS * HEAD_DIM
                tables = rope_tables(pos[grp][:min(rows, S)], ROT_DIM)
                p = linear_rope(x, b_w_in[j], mix_norm_g[i], tables, 2 * n_dil, n_dil)
                kv = p[:, n_dil:].reshape(nb, nt, 2, DIL_GROUPS, DIL_HEADS, HEAD_DIM)
                if grp == 0:
                    o = dilated_attention_prompt(p.reshape(nb, nt, 3 * n_dil)).reshape(rows, -1)
                else:
                    o = dilated_attention_sample(p[:, :n_dil].reshape(rows, DIL_GROUPS, DIL_HEADS, HEAD_DIM),
                                                 kv[:, 0], b_caches, j).reshape(rows, -1)
                mixed.append(linear(o, b_w_o[j], residual=x))
                for g, (win, _) in enumerate(DIL_PATTERNS):
                    b_out[g][grp].append(kv[:, nt - min(win, nt):, :, g])
            elif m == 2:
                nq, nkv = FOX_HEADS * HEAD_DIM, FOX_KV_HEADS * HEAD_DIM
                rep = FOX_HEADS // FOX_KV_HEADS
                p = linear(x, c_w_in[j], norm_g=mix_norm_g[i])
                kv = p[:, nq:nq + 2 * nkv].reshape(nb, nt, 2, FOX_KV_HEADS, HEAD_DIM)
                logf = jax.nn.log_sigmoid(p[:, nq + 2 * nkv:] + c_b_f[j]).reshape(nb, nt, FOX_HEADS)
                if grp == 0:
                    o = fox_attention_prompt(p.reshape(nb, nt, -1), jnp.cumsum(logf, axis=1)).reshape(rows, -1)
                else:
                    o = paged_decode("fox", p[:, :nq].reshape(rows, FOX_HEADS, HEAD_DIM),
                                     jnp.repeat(kv[:, 0, 0], rep, axis=1), jnp.repeat(kv[:, 0, 1], rep, axis=1),
                                     cache_c_kv, page_table, j, logf.reshape(rows, FOX_HEADS, 1),
                                     jnp.swapaxes(cache_c_logf, 2, 3)).reshape(rows, -1)
                mixed.append(linear(o, c_w_o[j], residual=x))
                ckv_out[grp].append(kv)
                clf_out[grp].append(logf)
            else:
                nq, nkv = SB_HEADS * HEAD_DIM, SB_KV_HEADS * HEAD_DIM
                rep = SB_HEADS // SB_KV_HEADS
                p = linear(x, d_w_in[j], norm_g=mix_norm_g[i])
                kv = p[:, nq:nq + 2 * nkv].reshape(nb, nt, 2, SB_KV_HEADS, HEAD_DIM)
                if grp == 0:
                    o = sb_attention_prompt(p.reshape(nb, nt, -1)).reshape(rows, -1)
                else:
                    o = paged_decode("sb", p[:, :nq].reshape(rows, SB_HEADS, HEAD_DIM),
                                     jnp.repeat(kv[:, 0, 0], rep, axis=1), jnp.repeat(kv[:, 0, 1], rep, axis=1),
                                     cache_d_kv, page_table, j).reshape(rows, -1)
                mixed.append(linear(o, d_w_o[j], residual=x))
                dkv_out[grp].append(kv)
        xs2 = [ffn_half(x, ffn_norm_g[i, 1], ffn_w_gu[i, 1], ffn_w_down[i, 1]) for x in mixed]

    y_prompt = rms_norm(xs2[0], final_norm_g).reshape(bp, S, D_MODEL)
    y_sample = rms_norm(xs2[1], final_norm_g).reshape(bs, T, D_MODEL)
    return (y_prompt, y_sample, jnp.stack(a_out[0]), jnp.stack(a_out[1]),
            jnp.stack(b_out[0][0]), jnp.stack(b_out[0][1]), jnp.stack(b_out[1][0]), jnp.stack(b_out[1][1]),
            jnp.stack(b_out[2][0]), jnp.stack(b_out[2][1]),
            jnp.stack(ckv_out[0]), jnp.stack(ckv_out[1]), jnp.stack(clf_out[0]), jnp.stack(clf_out[1]),
            jnp.stack(dkv_out[0]), jnp.stack(dkv_out[1]))


def _old_kernel_unused(x_prompt, x_sample, cache_a_latent, cache_b_kv_w128, cache_b_kv_w512, cache_b_kv_w2048,
           cache_c_kv, cache_c_logf, cache_d_kv, page_table,
           ffn_norm_g, ffn_w_gu, ffn_w_down, mix_norm_g,
           a_w_down, a_q_norm_g, a_kv_norm_g, a_w_uq, a_w_uk, a_w_uv, a_w_o,
           b_w_in, b_w_o, c_w_in, c_b_f, c_w_o, d_w_in, d_w_o, final_norm_g):
    DEPTH = ffn_norm_g.shape[0]
    S = x_prompt.shape[1]
    T = x_sample.shape[1]
    P = page_table.shape[1] * PAGE_SIZE
    pos_p = jnp.arange(S, dtype=jnp.int32)
    pos_s = P + jnp.arange(T, dtype=jnp.int32)
    kpos_s = jnp.arange(P + T, dtype=jnp.int32)
    b_caches = (cache_b_kv_w128, cache_b_kv_w512, cache_b_kv_w2048)

    a_p, a_s = [], []
    b_p = [[] for _ in range(DIL_GROUPS)]
    b_s = [[] for _ in range(DIL_GROUPS)]
    ckv_p, ckv_s, clf_p, clf_s = [], [], [], []
    dkv_p, dkv_s = [], []

    xp, xs = x_prompt, x_sample
    for i in range(DEPTH):
        m, j = i % N_MIXERS, i // N_MIXERS
        xp = macaron_half(xp, ffn_norm_g[i, 0], ffn_w_gu[i, 0], ffn_w_down[i, 0])
        xs = macaron_half(xs, ffn_norm_g[i, 0], ffn_w_gu[i, 0], ffn_w_down[i, 0])
        hp = rms_norm(xp, mix_norm_g[i])
        hs = rms_norm(xs, mix_norm_g[i])
        if m == 0:
            qn_p, qr_p, lat_p = mla_project(hp, pos_p, a_w_down[j], a_q_norm_g[j], a_kv_norm_g[j], a_w_uq[j])
            qn_s, qr_s, lat_s = mla_project(hs, pos_s, a_w_down[j], a_q_norm_g[j], a_kv_norm_g[j], a_w_uq[j])
            bp, bs = xp.shape[0], xs.shape[0]
            q_lat_p = jnp.einsum('bthn,chn->bthc', qn_p, a_w_uk[j]).reshape(bp, S, MLA_HEADS * MLA_KV_LORA)
            o_lat_p = mla_attention_prompt(q_lat_p, qr_p.reshape(bp, S, MLA_HEADS * MLA_ROPE), lat_p)
            o_p = jnp.einsum('bthc,chd->bthd', o_lat_p.reshape(bp, S, MLA_HEADS, MLA_KV_LORA), a_w_uv[j])
            mp = o_p.reshape(bp, S, MLA_HEADS * MLA_V) @ a_w_o[j]
            q_lat_s = jnp.einsum('bthn,chn->bthc', qn_s, a_w_uk[j])
            q_all_s = jnp.concatenate([q_lat_s, qr_s], axis=-1)[:, 0]
            k_new = jnp.broadcast_to(lat_s, (bs, MLA_HEADS, MLA_KV_LORA + MLA_ROPE))
            o_lat_s = paged_decode("mla", q_all_s, k_new, k_new[..., :MLA_KV_LORA], cache_a_latent, page_table, j)
            o_s = jnp.einsum('bhc,chd->bhd', o_lat_s, a_w_uv[j])
            ms = o_s.reshape(bs, T, MLA_HEADS * MLA_V) @ a_w_o[j]
            a_p.append(lat_p)
            a_s.append(lat_s)
        elif m == 1:
            q_p, kv_p = dilated_project(hp, pos_p, b_w_in[j])
            q_s, kv_s = dilated_project(hs, pos_s, b_w_in[j])
            for g, (win, dil) in enumerate(DIL_PATTERNS):
                b_p[g].append(kv_p[:, S - min(win, S):, :, g])
                b_s[g].append(kv_s[:, :, :, g])
            n_dil = DIL_GROUPS * DIL_HEADS * HEAD_DIM
            proj_p = hp @ b_w_in[j]
            o_p = dilated_attention_prompt(q_p.reshape(xp.shape[0], S, n_dil),
                                           kv_p[:, :, 0].reshape(xp.shape[0], S, n_dil), proj_p)
            mp = o_p @ b_w_o[j]
            o_s = dilated_attention_sample(q_s[:, 0], kv_s[:, 0], b_caches, j)
            ms = o_s.reshape(xs.shape[0], T, DIL_HEADS * HEAD_DIM) @ b_w_o[j]
        elif m == 2:
            nq_c, nkv_c = FOX_HEADS * HEAD_DIM, FOX_KV_HEADS * HEAD_DIM
            proj_p = hp @ c_w_in[j]
            kv_p = proj_p[..., nq_c:nq_c + 2 * nkv_c].reshape(xp.shape[0], S, 2, FOX_KV_HEADS, HEAD_DIM)
            lf_p = jax.nn.log_sigmoid((proj_p[..., nq_c + 2 * nkv_c:] + c_b_f[j]).astype(jnp.float32))
            q_s, kv_s, lf_s = fox_project(hs, c_w_in[j], c_b_f[j])
            mp = fox_attention_prompt(proj_p, jnp.cumsum(lf_p, axis=1)) @ c_w_o[j]
            rep_c = FOX_HEADS // FOX_KV_HEADS
            o_s = paged_decode("fox", q_s[:, 0], jnp.repeat(kv_s[:, 0, 0], rep_c, axis=1),
                               jnp.repeat(kv_s[:, 0, 1], rep_c, axis=1), cache_c_kv, page_table, j,
                               lf_s.reshape(xs.shape[0], FOX_HEADS, 1), jnp.swapaxes(cache_c_logf, 2, 3))
            ms = o_s.reshape(xs.shape[0], T, FOX_HEADS * HEAD_DIM) @ c_w_o[j]
            ckv_p.append(kv_p)
            ckv_s.append(kv_s)
            clf_p.append(lf_p)
            clf_s.append(lf_s)
        else:
            nq_d, nkv_d = SB_HEADS * HEAD_DIM, SB_KV_HEADS * HEAD_DIM
            proj_p = hp @ d_w_in[j]
            kv_p = proj_p[..., nq_d:nq_d + 2 * nkv_d].reshape(xp.shape[0], S, 2, SB_KV_HEADS, HEAD_DIM)
            q_s, kv_s = sb_project(hs, d_w_in[j])
            mp = sb_attention_prompt(proj_p) @ d_w_o[j]
            rep_d = SB_HEADS // SB_KV_HEADS
            o_s = paged_decode("sb", q_s[:, 0], jnp.repeat(kv_s[:, 0, 0], rep_d, axis=1),
                               jnp.repeat(kv_s[:, 0, 1], rep_d, axis=1), cache_d_kv, page_table, j)
            ms = o_s.reshape(xs.shape[0], T, SB_HEADS * HEAD_DIM) @ d_w_o[j]
            dkv_p.append(kv_p)
            dkv_s.append(kv_s)
        xp = xp + mp.astype(xp.dtype)
        xs = xs + ms.astype(xs.dtype)
        xp = macaron_half(xp, ffn_norm_g[i, 1], ffn_w_gu[i, 1], ffn_w_down[i, 1])
        xs = macaron_half(xs, ffn_norm_g[i, 1], ffn_w_gu[i, 1], ffn_w_down[i, 1])

    y_prompt = rms_norm(xp, final_norm_g)
    y_sample = rms_norm(xs, final_norm_g)
    return (y_prompt, y_sample, jnp.stack(a_p), jnp.stack(a_s),
            jnp.stack(b_p[0]), jnp.stack(b_s[0]), jnp.stack(b_p[1]), jnp.stack(b_s[1]),
            jnp.stack(b_p[2]), jnp.stack(b_s[2]),
            jnp.stack(ckv_p), jnp.stack(ckv_s), jnp.stack(clf_p), jnp.stack(clf_s),
            jnp.stack(dkv_p), jnp.stack(dkv_s))
```

```python
import functools

import jax
import jax.numpy as jnp
from jax import lax
from jax.experimental import pallas as pl
from jax.experimental.pallas import tpu as pltpu

D_MODEL = 1024
PAGE_SIZE = 128
N_MIXERS = 4
HEAD_DIM = 128
N_HEADS = D_MODEL // HEAD_DIM
ROPE_THETA = 500000.0
ROT_DIM = HEAD_DIM // 4
RMS_EPS = 1e-6
D_FF = 2816

MLA_HEADS = N_HEADS
MLA_NOPE = 128
MLA_ROPE = 64
MLA_V = 128
MLA_Q_LORA = 384
MLA_KV_LORA = 256
DIL_PATTERNS = ((128, 1), (512, 4), (2048, 16))
DIL_GROUPS = 3
DIL_HEADS = N_HEADS
FOX_HEADS = N_HEADS
FOX_KV_HEADS = 2
SB_HEADS = N_HEADS
SB_KV_HEADS = 2

V7X_VMEM_LIMIT_BYTES = 56 * 1024 * 1024


def _ffn_kernel(x_ref, g_ref, wgu_ref, wd_ref, o_ref):
    x = x_ref[...]
    ms = jnp.mean(x * x, axis=-1, keepdims=True)
    h = (x * lax.rsqrt(ms + RMS_EPS) * g_ref[...]).astype(jnp.bfloat16)
    gu = jnp.dot(h, wgu_ref[...], preferred_element_type=jnp.float32)
    gate = gu[:, :D_FF]
    up = gu[:, D_FF:]
    act = (gate / (1.0 + jnp.exp(-gate)) * up).astype(jnp.bfloat16)
    o_ref[...] = x + 0.5 * jnp.dot(act, wd_ref[...], preferred_element_type=jnp.float32)


def ffn_half(x2d, g, w_gu, w_down):
    m = x2d.shape[0]
    tm = min(m, 512)
    assert m % tm == 0
    resident = pl.Buffered(1)
    return pl.pallas_call(
        _ffn_kernel,
        grid=(m // tm,),
        in_specs=[
            pl.BlockSpec((tm, D_MODEL), lambda i: (i, 0)),
            pl.BlockSpec((1, D_MODEL), lambda i: (0, 0), pipeline_mode=resident),
            pl.BlockSpec((D_MODEL, 2 * D_FF), lambda i: (0, 0), pipeline_mode=resident),
            pl.BlockSpec((D_FF, D_MODEL), lambda i: (0, 0), pipeline_mode=resident),
        ],
        out_specs=pl.BlockSpec((tm, D_MODEL), lambda i: (i, 0)),
        out_shape=jax.ShapeDtypeStruct((m, D_MODEL), jnp.float32),
        compiler_params=pltpu.CompilerParams(
            dimension_semantics=("arbitrary",), vmem_limit_bytes=V7X_VMEM_LIMIT_BYTES),
        name="ffn_half",
    )(x2d, g.reshape(1, D_MODEL), w_gu.astype(jnp.bfloat16), w_down.astype(jnp.bfloat16))


def _linear_kernel(*refs, has_norm, has_res):
    refs = list(refs)
    x_ref = refs.pop(0)
    g_ref = refs.pop(0) if has_norm else None
    w_ref = refs.pop(0)
    r_ref = refs.pop(0) if has_res else None
    o_ref = refs.pop(0)
    x = x_ref[...]
    if has_norm:
        ms = jnp.mean(x * x, axis=-1, keepdims=True)
        x = x * lax.rsqrt(ms + RMS_EPS) * g_ref[...]
    y = jnp.dot(x.astype(jnp.bfloat16), w_ref[...], preferred_element_type=jnp.float32)
    if has_res:
        y = r_ref[...] + y
    o_ref[...] = y


def _pick_tile(n, cap):
    if n <= cap:
        return n
    best = n
    for t in range(128, cap + 1, 128):
        if n % t == 0:
            best = t
    return best


def linear(x2d, w, norm_g=None, residual=None):
    m, kdim = x2d.shape
    n = w.shape[1]
    tm = _pick_tile(m, 512)
    tn = _pick_tile(n, 2304)
    assert m % tm == 0 and n % tn == 0
    in_specs = [pl.BlockSpec((tm, kdim), lambda jn, im: (im, 0))]
    args = [x2d]
    if norm_g is not None:
        in_specs.append(pl.BlockSpec((1, kdim), lambda jn, im: (0, 0)))
        args.append(norm_g.reshape(1, kdim))
    in_specs.append(pl.BlockSpec((kdim, tn), lambda jn, im: (0, jn)))
    args.append(w.astype(jnp.bfloat16))
    if residual is not None:
        in_specs.append(pl.BlockSpec((tm, tn), lambda jn, im: (im, jn)))
        args.append(residual)
    return pl.pallas_call(
        functools.partial(_linear_kernel, has_norm=norm_g is not None, has_res=residual is not None),
        grid=(n // tn, m // tm),
        in_specs=in_specs,
        out_specs=pl.BlockSpec((tm, tn), lambda jn, im: (im, jn)),
        out_shape=jax.ShapeDtypeStruct((m, n), jnp.float32),
        compiler_params=pltpu.CompilerParams(
            dimension_semantics=("arbitrary", "arbitrary"), vmem_limit_bytes=V7X_VMEM_LIMIT_BYTES),
        name="linear",
    )(*args)


def _linear_rope_kernel(x_ref, g_ref, w_ref, cos_ref, sin_lo_ref, sin_hi_ref, o_ref, *, rope_tiles, shift):
    jn = pl.program_id(0)
    x = x_ref[...]
    ms = jnp.mean(x * x, axis=-1, keepdims=True)
    x = x * lax.rsqrt(ms + RMS_EPS) * g_ref[...]
    y = jnp.dot(x.astype(jnp.bfloat16), w_ref[...], preferred_element_type=jnp.float32)

    @pl.when(jn >= rope_tiles)
    def _():
        o_ref[...] = y

    @pl.when(jn < rope_tiles)
    def _():
        cos, sin_lo, sin_hi = cos_ref[...], sin_lo_ref[...], sin_hi_ref[...]
        for h in range(y.shape[1] // HEAD_DIM):
            yh = y[:, h * HEAD_DIM:(h + 1) * HEAD_DIM]
            ahead = pltpu.roll(yh, HEAD_DIM - shift, 1)
            behind = pltpu.roll(yh, shift, 1)
            o_ref[:, h * HEAD_DIM:(h + 1) * HEAD_DIM] = yh * cos + ahead * sin_lo + behind * sin_hi


def rope_tables(pos, rot_dim):
    half = rot_dim // 2
    inv_freq = jnp.power(ROPE_THETA, -jnp.arange(half, dtype=jnp.float32) * 2.0 / rot_dim)
    ang = pos.astype(jnp.float32)[:, None] * inv_freq[None, :]
    cos, sin = jnp.cos(ang), jnp.sin(ang)
    r = pos.shape[0]
    zeros = lambda w: jnp.zeros((r, w), jnp.float32)
    cos_t = jnp.concatenate([cos, cos, jnp.ones((r, HEAD_DIM - rot_dim), jnp.float32)], axis=1)
    sin_lo = jnp.concatenate([-sin, zeros(HEAD_DIM - half)], axis=1)
    sin_hi = jnp.concatenate([zeros(half), sin, zeros(HEAD_DIM - rot_dim)], axis=1)
    return cos_t, sin_lo, sin_hi


def linear_rope(x2d, w, norm_g, tables, rope_cols, tn):
    m, kdim = x2d.shape
    n = w.shape[1]
    tm = _pick_tile(m, 512)
    period = tables[0].shape[0]
    assert m % tm == 0 and n % tn == 0 and rope_cols % tn == 0 and period % tm == 0
    table_spec = pl.BlockSpec((tm, HEAD_DIM), lambda jn, im: (im % (period // tm), 0))
    return pl.pallas_call(
        functools.partial(_linear_rope_kernel, rope_tiles=rope_cols // tn, shift=ROT_DIM // 2),
        grid=(n // tn, m // tm),
        in_specs=[pl.BlockSpec((tm, kdim), lambda jn, im: (im, 0)),
                  pl.BlockSpec((1, kdim), lambda jn, im: (0, 0)),
                  pl.BlockSpec((kdim, tn), lambda jn, im: (0, jn)),
                  table_spec, table_spec, table_spec],
        out_specs=pl.BlockSpec((tm, tn), lambda jn, im: (im, jn)),
        out_shape=jax.ShapeDtypeStruct((m, n), jnp.float32),
        compiler_params=pltpu.CompilerParams(
            dimension_semantics=("arbitrary", "arbitrary"), vmem_limit_bytes=V7X_VMEM_LIMIT_BYTES),
        name="linear_rope",
    )(x2d, norm_g.reshape(1, kdim), w.astype(jnp.bfloat16), *tables)


def _headwise_kernel(x_ref, w_ref, o_ref, *, n_heads, din, dout):
    for h in range(n_heads):
        xh = x_ref[:, h * din:(h + 1) * din].astype(jnp.bfloat16)
        o_ref[:, h * dout:(h + 1) * dout] = jnp.dot(xh, w_ref[h], preferred_element_type=jnp.float32)


def headwise_matmul(x2d, w):
    m = x2d.shape[0]
    n_heads, din, dout = w.shape
    tm = _pick_tile(m, 512)
    return pl.pallas_call(
        functools.partial(_headwise_kernel, n_heads=n_heads, din=din, dout=dout),
        grid=(m // tm,),
        in_specs=[pl.BlockSpec((tm, n_heads * din), lambda i: (i, 0)),
                  pl.BlockSpec((n_heads, din, dout), lambda i: (0, 0, 0))],
        out_specs=pl.BlockSpec((tm, n_heads * dout), lambda i: (i, 0)),
        out_shape=jax.ShapeDtypeStruct((m, n_heads * dout), jnp.float32),
        compiler_params=pltpu.CompilerParams(
            dimension_semantics=("arbitrary",), vmem_limit_bytes=V7X_VMEM_LIMIT_BYTES),
        name="headwise_matmul",
    )(x2d, w.astype(jnp.bfloat16))


ATTN_BLOCK = 512


def _causal_mask(tq, tk, strict):
    row = lax.broadcasted_iota(jnp.int32, (tq, tk), 0)
    col = lax.broadcasted_iota(jnp.int32, (tq, tk), 1)
    return (col < row) if strict else (col <= row)


def _dot_nt(a, b):
    return lax.dot_general(a, b, (((1,), (1,)), ((), ())), preferred_element_type=jnp.float32)


def _softmax_heads(score_fn, v_ref, o_ref, n_heads, dv, blk):
    qi = pl.program_id(2)
    mask = _causal_mask(blk, blk, strict=False)

    def diagonal(h):
        s = jnp.where(mask, score_fn(h, qi), -1e30)
        m0 = jnp.max(s, axis=-1, keepdims=True)
        p = jnp.exp(s - m0)
        l0 = jnp.sum(p, axis=-1, keepdims=True)
        vd = v_ref[pl.ds(pl.multiple_of(qi * blk, blk), blk), :]
        return m0, l0, jnp.dot(p.astype(jnp.bfloat16), vd, preferred_element_type=jnp.float32)

    def past_block(h, kb, carry):
        m, l, acc = carry
        s = score_fn(h, kb)
        m_new = jnp.maximum(m, jnp.max(s, axis=-1, keepdims=True))
        alpha = jnp.exp(m - m_new)
        p = jnp.exp(s - m_new)
        l = alpha * l + jnp.sum(p, axis=-1, keepdims=True)
        vb = v_ref[pl.ds(pl.multiple_of(kb * blk, blk), blk), :]
        acc = alpha * acc + jnp.dot(p.astype(jnp.bfloat16), vb, preferred_element_type=jnp.float32)
        return m_new, l, acc

    def body(j, carry):
        return tuple(past_block(h, qi - 1 - j, carry[h]) for h in range(n_heads))

    final = lax.fori_loop(0, qi, body, tuple(diagonal(h) for h in range(n_heads)))
    for h, (m, l, acc) in enumerate(final):
        o_ref[:, h * dv:(h + 1) * dv] = acc / l


def _fox_prompt_kernel(q_ref, k_ref, v_ref, cq_ref, ck_ref, o_ref, kb_ref, vb_ref, *, rep, blk):
    qi = pl.program_id(2)

    @pl.when(qi == 0)
    def _():
        kb_ref[...] = k_ref[...].astype(jnp.bfloat16)
        vb_ref[...] = v_ref[...].astype(jnp.bfloat16)

    scale = HEAD_DIM ** -0.5
    qs = [(q_ref[:, h * HEAD_DIM:(h + 1) * HEAD_DIM] * scale).astype(jnp.bfloat16) for h in range(rep)]

    def score(h, kb):
        start = pl.multiple_of(kb * blk, blk)
        kblk = kb_ref[pl.ds(start, blk), :]
        cq = cq_ref[:, h:h + 1]
        ck = ck_ref[pl.ds(h, 1), pl.ds(start, blk)]
        return _dot_nt(qs[h], kblk) + cq - ck

    _softmax_heads(score, vb_ref, o_ref, rep, HEAD_DIM, blk)


def fox_attention_prompt(p, c):
    b, t, _ = p.shape
    blk = min(ATTN_BLOCK, t)
    rep = FOX_HEADS // FOX_KV_HEADS
    nqb = FOX_HEADS
    ck = jnp.transpose(c, (0, 2, 1)).reshape(b, FOX_KV_HEADS, rep, t)
    cq = jnp.transpose(c.reshape(b, t, FOX_KV_HEADS, rep), (0, 2, 1, 3))
    return pl.pallas_call(
        functools.partial(_fox_prompt_kernel, rep=rep, blk=blk),
        grid=(b, FOX_KV_HEADS, t // blk),
        in_specs=[
            pl.BlockSpec((None, blk, rep * HEAD_DIM), lambda i, g, q: (i, q, g)),
            pl.BlockSpec((None, t, HEAD_DIM), lambda i, g, q: (i, 0, nqb + g)),
            pl.BlockSpec((None, t, HEAD_DIM), lambda i, g, q: (i, 0, nqb + FOX_KV_HEADS + g)),
            pl.BlockSpec((None, None, blk, rep), lambda i, g, q: (i, g, q, 0)),
            pl.BlockSpec((None, None, rep, t), lambda i, g, q: (i, g, 0, 0)),
        ],
        out_specs=pl.BlockSpec((None, blk, rep * HEAD_DIM), lambda i, g, q: (i, q, g)),
        out_shape=jax.ShapeDtypeStruct((b, t, FOX_HEADS * HEAD_DIM), jnp.float32),
        scratch_shapes=[pltpu.VMEM((t, HEAD_DIM), jnp.bfloat16), pltpu.VMEM((t, HEAD_DIM), jnp.bfloat16)],
        compiler_params=pltpu.CompilerParams(
            dimension_semantics=("arbitrary", "arbitrary", "arbitrary"), vmem_limit_bytes=V7X_VMEM_LIMIT_BYTES),
        name="fox_attention_prompt",
    )(p, p, p, cq, ck)


def _mla_prompt_kernel(ql_ref, qr_ref, lat_ref, o_ref, kb_ref, *, blk):
    qi = pl.program_id(2)

    @pl.when(qi == 0)
    def _():
        kb_ref[...] = lat_ref[...].astype(jnp.bfloat16)

    scale = (MLA_NOPE + MLA_ROPE) ** -0.5
    ql = [(ql_ref[:, h * MLA_KV_LORA:(h + 1) * MLA_KV_LORA] * scale).astype(jnp.bfloat16) for h in range(MLA_HEADS)]
    qr = [(qr_ref[:, h * MLA_ROPE:(h + 1) * MLA_ROPE] * scale).astype(jnp.bfloat16) for h in range(MLA_HEADS)]

    def score(h, kb):
        start = pl.multiple_of(kb * blk, blk)
        k_lat = kb_ref[pl.ds(start, blk), :MLA_KV_LORA]
        k_rope = kb_ref[pl.ds(start, blk), MLA_KV_LORA:]
        return _dot_nt(ql[h], k_lat) + _dot_nt(qr[h], k_rope)

    _softmax_heads(score, kb_ref.at[:, :MLA_KV_LORA], o_ref, MLA_HEADS, MLA_KV_LORA, blk)


def mla_attention_prompt(q_lat, q_rope, latent):
    b, t, _ = q_lat.shape
    blk = min(ATTN_BLOCK, t)
    dl = MLA_KV_LORA + MLA_ROPE
    return pl.pallas_call(
        functools.partial(_mla_prompt_kernel, blk=blk),
        grid=(b, 1, t // blk),
        in_specs=[
            pl.BlockSpec((None, blk, MLA_HEADS * MLA_KV_LORA), lambda i, g, q: (i, q, 0)),
            pl.BlockSpec((None, blk, MLA_HEADS * MLA_ROPE), lambda i, g, q: (i, q, 0)),
            pl.BlockSpec((None, t, dl), lambda i, g, q: (i, 0, 0)),
        ],
        out_specs=pl.BlockSpec((None, blk, MLA_HEADS * MLA_KV_LORA), lambda i, g, q: (i, q, 0)),
        out_shape=jax.ShapeDtypeStruct((b, t, MLA_HEADS * MLA_KV_LORA), jnp.float32),
        scratch_shapes=[pltpu.VMEM((t, dl), jnp.bfloat16)],
        compiler_params=pltpu.CompilerParams(
            dimension_semantics=("arbitrary", "arbitrary", "arbitrary"), vmem_limit_bytes=V7X_VMEM_LIMIT_BYTES),
        name="mla_attention_prompt",
    )(q_lat, q_rope, latent)


def _sb_prompt_kernel(q_ref, k_ref, v_ref, o_ref, kb_ref, vb_ref, *, rep, blk):
    qi = pl.program_id(2)

    @pl.when(qi == 0)
    def _():
        kb_ref[...] = k_ref[...].astype(jnp.bfloat16)
        vb_ref[...] = v_ref[...].astype(jnp.bfloat16)

    scale = HEAD_DIM ** -0.5
    mask = _causal_mask(blk, blk, strict=True)
    row = lax.broadcasted_iota(jnp.int32, (blk, blk), 0)
    col = lax.broadcasted_iota(jnp.int32, (blk, blk), 1)
    later_sel = jnp.where(row > col, 1.0, 0.0).astype(jnp.bfloat16)

    def block(qh, kb, carry, masked):
        start = pl.multiple_of(kb * blk, blk)
        z = _dot_nt(qh, kb_ref[pl.ds(start, blk), :])
        sp = jnp.log(1.0 + jnp.exp(-jnp.abs(z)))
        log_beta = jnp.minimum(z, 0.0) - sp
        log_keep = jnp.minimum(-z, 0.0) - sp
        if masked:
            log_keep = jnp.where(mask, log_keep, 0.0)
        hi = log_keep.astype(jnp.bfloat16)
        lo = (log_keep - hi.astype(jnp.float32)).astype(jnp.bfloat16)
        later = (jnp.dot(hi, later_sel, preferred_element_type=jnp.float32)
                 + jnp.dot(lo, later_sel, preferred_element_type=jnp.float32))
        if carry is not None:
            later = later + carry[0]
        a = jnp.exp(log_beta + later)
        if masked:
            a = jnp.where(mask, a, 0.0)
        pv = jnp.dot(a.astype(jnp.bfloat16), vb_ref[pl.ds(start, blk), :], preferred_element_type=jnp.float32)
        tail = jnp.sum(log_keep, axis=-1, keepdims=True)
        if carry is None:
            return tail, pv
        return carry[0] + tail, carry[1] + pv

    qs = [(q_ref[:, h * HEAD_DIM:(h + 1) * HEAD_DIM] * scale).astype(jnp.bfloat16) for h in range(rep)]

    def body(j, carry):
        return tuple(block(qs[h], qi - 1 - j, carry[h], False) for h in range(rep))

    final = lax.fori_loop(0, qi, body, tuple(block(qs[h], qi, None, True) for h in range(rep)))
    for h, (_, acc) in enumerate(final):
        o_ref[:, h * HEAD_DIM:(h + 1) * HEAD_DIM] = acc


def sb_attention_prompt(p):
    b, t, _ = p.shape
    blk = min(ATTN_BLOCK, t)
    rep = SB_HEADS // SB_KV_HEADS
    nqb = SB_HEADS
    return pl.pallas_call(
        functools.partial(_sb_prompt_kernel, rep=rep, blk=blk),
        grid=(b, SB_KV_HEADS, t // blk),
        in_specs=[
            pl.BlockSpec((None, blk, rep * HEAD_DIM), lambda i, g, q: (i, q, g)),
            pl.BlockSpec((None, t, HEAD_DIM), lambda i, g, q: (i, 0, nqb + g)),
            pl.BlockSpec((None, t, HEAD_DIM), lambda i, g, q: (i, 0, nqb + SB_KV_HEADS + g)),
        ],
        out_specs=pl.BlockSpec((None, blk, rep * HEAD_DIM), lambda i, g, q: (i, q, g)),
        out_shape=jax.ShapeDtypeStruct((b, t, SB_HEADS * HEAD_DIM), jnp.float32),
        scratch_shapes=[pltpu.VMEM((t, HEAD_DIM), jnp.bfloat16), pltpu.VMEM((t, HEAD_DIM), jnp.bfloat16)],
        compiler_params=pltpu.CompilerParams(
            dimension_semantics=("arbitrary", "arbitrary", "arbitrary"), vmem_limit_bytes=V7X_VMEM_LIMIT_BYTES),
        name="sb_attention_prompt",
    )(p, p, p)


DIL_N = 128
DIL_UNROLL = 4


def _dilated_prompt_kernel(*refs, t):
    qkv = refs[:9]
    o_ref, og_ref, lse_ref = refs[9:]
    scale = HEAD_DIM ** -0.5
    n = DIL_N
    qi = lax.broadcasted_iota(jnp.int32, (n, 2 * n), 0)
    kj = lax.broadcasted_iota(jnp.int32, (n, 2 * n), 1)
    cur_ok = (kj >= n) & (kj - n <= qi)
    prev_ok = (kj < n) & (kj >= qi)

    for g, (_, dil) in enumerate(DIL_PATTERNS):
        q_ref, k_ref, v_ref = qkv[3 * g:3 * g + 3]
        nb = t // (dil * n)

        def unit(u, q_ref=q_ref, k_ref=k_ref, v_ref=v_ref, dil=dil, nb=nb, g=g):
            r, i = u // nb, u % nb
            cur = r + dil * n * i
            prev = r + dil * n * jnp.maximum(i - 1, 0)
            rows = lambda ref, start: ref[pl.ds(start, n, stride=dil), :]
            qb = (rows(q_ref, cur) * scale).astype(jnp.bfloat16)
            kb = jnp.concatenate([rows(k_ref, prev), rows(k_ref, cur)], axis=0).astype(jnp.bfloat16)
            vb = jnp.concatenate([rows(v_ref, prev), rows(v_ref, cur)], axis=0).astype(jnp.bfloat16)
            s = _dot_nt(qb, kb)
            s = jnp.where(cur_ok | (prev_ok & (i > 0)), s, -1e30)
            m = jnp.max(s, axis=-1, keepdims=True)
            p = jnp.exp(s - m)
            l = jnp.sum(p, axis=-1, keepdims=True)
            o = jnp.dot(p.astype(jnp.bfloat16), vb, preferred_element_type=jnp.float32) / l
            og_ref.at[g][pl.ds(cur, n, stride=dil), :] = o
            lse_ref.at[g][pl.ds(cur, n, stride=dil), :] = jnp.broadcast_to(m + jnp.log(l), (n, HEAD_DIM))

        def body(j, carry, unit=unit):
            for uu in range(DIL_UNROLL):
                unit(j * DIL_UNROLL + uu)
            return carry

        lax.fori_loop(0, t // n // DIL_UNROLL, body, 0)

    lses = [lse_ref[g] for g in range(DIL_GROUPS)]
    top = functools.reduce(jnp.maximum, lses)
    ws = [jnp.exp(x - top) for x in lses]
    den = functools.reduce(lambda a, b: a + b, ws)
    o_ref[...] = functools.reduce(lambda a, b: a + b, [w * og_ref[g] for g, w in enumerate(ws)]) / den


def dilated_attention_prompt(p):
    b, t, _ = p.shape
    assert all(t % (dil * DIL_N) == 0 for _, dil in DIL_PATTERNS)
    n_gh = DIL_GROUPS * DIL_HEADS
    in_specs, args = [], []
    for g in range(DIL_GROUPS):
        for part in range(3):
            in_specs.append(pl.BlockSpec((None, t, HEAD_DIM),
                                         lambda i, h, g=g, part=part: (i, 0, part * n_gh + g * DIL_HEADS + h)))
        args += [p, p, p]
    return pl.pallas_call(
        functools.partial(_dilated_prompt_kernel, t=t),
        grid=(b, DIL_HEADS),
        in_specs=in_specs,
        out_specs=pl.BlockSpec((None, t, HEAD_DIM), lambda i, h: (i, 0, h)),
        out_shape=jax.ShapeDtypeStruct((b, t, DIL_HEADS * HEAD_DIM), jnp.float32),
        scratch_shapes=[pltpu.VMEM((DIL_GROUPS, t, HEAD_DIM), jnp.float32),
                        pltpu.VMEM((DIL_GROUPS, t, HEAD_DIM), jnp.float32)],
        compiler_params=pltpu.CompilerParams(
            dimension_semantics=("arbitrary", "arbitrary"), vmem_limit_bytes=V7X_VMEM_LIMIT_BYTES),
        name="dilated_attention_prompt",
    )(*args)


def _dilated_sample_kernel(q_ref, kvn_ref, c0_ref, c1_ref, c2_ref, o_ref):
    scale = HEAD_DIM ** -0.5
    outs, lses = [], []
    for g, c_ref in enumerate((c0_ref, c1_ref, c2_ref)):
        qg = q_ref[g] * scale
        kc, vc = c_ref[:, 0], c_ref[:, 1]
        kn, vn = kvn_ref[0, g], kvn_ref[1, g]
        s_c = jnp.sum(kc * qg[None], axis=-1, keepdims=True)
        s_n = jnp.sum(kn * qg, axis=-1, keepdims=True)
        m = jnp.maximum(jnp.max(s_c, axis=0), s_n)
        p_c = jnp.exp(s_c - m[None])
        p_n = jnp.exp(s_n - m)
        l = jnp.sum(p_c, axis=0) + p_n
        outs.append((jnp.sum(p_c * vc, axis=0) + p_n * vn) / l)
        lses.append(m + jnp.log(l))
    top = functools.reduce(jnp.maximum, lses)
    ws = [jnp.exp(x - top) for x in lses]
    den = functools.reduce(lambda a, b: a + b, ws)
    o_ref[...] = functools.reduce(lambda a, b: a + b, [w * o for w, o in zip(ws, outs)]) / den


def dilated_attention_sample(q, kv_new, caches, layer):
    bd = q.shape[0]
    in_specs = [pl.BlockSpec((None, DIL_GROUPS, DIL_HEADS, HEAD_DIM), lambda i: (i, 0, 0, 0)),
                pl.BlockSpec((None, 2, DIL_GROUPS, DIL_HEADS, HEAD_DIM), lambda i: (i, 0, 0, 0, 0))]
    args = [q, kv_new]
    for (win, dil), cache in zip(DIL_PATTERNS, caches):
        assert cache.shape[2] == win == DIL_N * dil
        args.append(cache.reshape(cache.shape[0], bd, DIL_N, dil, 2, DIL_HEADS, HEAD_DIM))
        in_specs.append(pl.BlockSpec((None, None, DIL_N, None, 2, DIL_HEADS, HEAD_DIM),
                                     lambda i: (layer, i, 0, 0, 0, 0, 0)))
    return pl.pallas_call(
        _dilated_sample_kernel,
        grid=(bd,),
        in_specs=in_specs,
        out_specs=pl.BlockSpec((None, DIL_HEADS, HEAD_DIM), lambda i: (i, 0, 0)),
        out_shape=jax.ShapeDtypeStruct((bd, DIL_HEADS, HEAD_DIM), jnp.float32),
        compiler_params=pltpu.CompilerParams(
            dimension_semantics=("arbitrary",), vmem_limit_bytes=V7X_VMEM_LIMIT_BYTES),
        name="dilated_attention_sample",
    )(*args)


DECODE_PAGES = 64


def _split3(x):
    hi = x.astype(jnp.bfloat16).astype(jnp.float32)
    r = x - hi
    mid = r.astype(jnp.bfloat16).astype(jnp.float32)
    lo = r - mid
    return hi, mid, lo


def _dot_split(xs, w_bf16):
    n = xs[0].shape[0]
    pieces = [piece for x in xs for piece in _split3(x)]
    r = jnp.dot(jnp.concatenate(pieces, axis=0).astype(jnp.bfloat16), w_bf16, preferred_element_type=jnp.float32)
    return [r[3 * i * n:(3 * i + 1) * n] + r[(3 * i + 1) * n:(3 * i + 2) * n] + r[(3 * i + 2) * n:(3 * i + 3) * n]
            for i in range(len(xs))]


def _decode_kernel(pt_ref, *refs, mode, n_pages_step, scale):
    del pt_ref
    np_ = n_pages_step
    if mode == "fox":
        q_ref, kn_ref, vn_ref, lfn_ref = refs[:4]
        rest = refs[4:]
        page_refs, lf_refs, rest = rest[:np_], rest[np_:2 * np_], rest[2 * np_:]
    else:
        q_ref, kn_ref, vn_ref = refs[:3]
        rest = refs[3:]
        page_refs, rest = rest[:np_], rest[np_:]
        lf_refs = None
    o_ref, m_sc, l_sc, acc_sc, r_sc = rest
    c = pl.program_id(1)
    n_chunks = pl.num_programs(1)
    gqa = mode != "mla"
    q = q_ref[...] * scale
    qb = q.astype(jnp.bfloat16)
    n_heads = q.shape[0]
    row = lax.broadcasted_iota(jnp.int32, (n_heads, PAGE_SIZE), 0)
    first_group = row < (n_heads // 2)

    @pl.when(c == 0)
    def _():
        if mode == "sb":
            acc_sc[...] = jnp.zeros_like(acc_sc)
            r_sc[...] = jnp.zeros_like(r_sc)
        else:
            m_sc[...] = jnp.sum(q * kn_ref[...], axis=-1, keepdims=True)
            l_sc[...] = jnp.ones_like(l_sc)
            acc_sc[...] = jnp.broadcast_to(vn_ref[...], acc_sc.shape)
            if mode == "fox":
                r_sc[...] = lfn_ref[...]

    def keys_values(k):
        if gqa:
            pr = page_refs[k]
            kcat = jnp.concatenate([pr[pl.ds(g, PAGE_SIZE, stride=4), :] for g in range(2)], axis=0)
            vcat = jnp.concatenate([pr[pl.ds(2 + g, PAGE_SIZE, stride=4), :] for g in range(2)], axis=0)
            return kcat.astype(jnp.bfloat16), vcat.astype(jnp.bfloat16)
        kt = page_refs[k][...].astype(jnp.bfloat16)
        return kt, kt[:MLA_KV_LORA]

    def qk(keys):
        if gqa:
            r = _dot_nt(qb, keys)
            return jnp.where(first_group, r[:, :PAGE_SIZE], r[:, PAGE_SIZE:])
        return jnp.dot(qb, keys, preferred_element_type=jnp.float32)

    def pv_dot(w, vals):
        if gqa:
            w = jnp.concatenate([jnp.where(first_group, w, 0.0), jnp.where(first_group, 0.0, w)], axis=1)
            return jnp.dot(w.astype(jnp.bfloat16), vals, preferred_element_type=jnp.float32)
        return _dot_nt(w.astype(jnp.bfloat16), vals)

    if mode in ("fox", "sb"):
        rr = lax.broadcasted_iota(jnp.int32, (PAGE_SIZE, PAGE_SIZE), 0)
        cc = lax.broadcasted_iota(jnp.int32, (PAGE_SIZE, PAGE_SIZE), 1)
        later_sel = jnp.where(rr > cc, 1.0, 0.0).astype(jnp.bfloat16)

    def suffix_terms(per_page):
        within = _dot_split(per_page, later_sel)
        run = r_sc[...]
        out = []
        for k in range(np_):
            out.append(within[k] + run)
            run = run + jnp.sum(per_page[k], axis=-1, keepdims=True)
        r_sc[...] = run
        return out

    kvs = [keys_values(k) for k in range(np_)]
    values = [kv[1] for kv in kvs]
    scores = [qk(kv[0]) for kv in kvs]
    if mode == "fox":
        bias = suffix_terms([lf_refs[k][...] for k in range(np_)])
        scores = [s + b for s, b in zip(scores, bias)]
    if mode == "sb":
        z = jnp.concatenate(scores, axis=1)
        sp = jnp.log(1.0 + jnp.exp(-jnp.abs(z)))
        log_beta = jnp.minimum(z, 0.0) - sp
        log_keep = jnp.minimum(-z, 0.0) - sp
        later = suffix_terms([log_keep[:, k * PAGE_SIZE:(k + 1) * PAGE_SIZE] for k in range(np_)])
        a = jnp.exp(log_beta + jnp.concatenate(later, axis=1))
        pv = None
        for k in range(np_):
            t = pv_dot(a[:, k * PAGE_SIZE:(k + 1) * PAGE_SIZE], values[k])
            pv = t if pv is None else pv + t
        acc_sc[...] = acc_sc[...] + pv
    else:
        s_all = jnp.concatenate(scores, axis=1)
        m_prev = m_sc[...]
        m_new = jnp.maximum(m_prev, jnp.max(s_all, axis=-1, keepdims=True))
        alpha = jnp.exp(m_prev - m_new)
        p_all = jnp.exp(s_all - m_new)
        l_sc[...] = alpha * l_sc[...] + jnp.sum(p_all, axis=-1, keepdims=True)
        pv = None
        for k in range(np_):
            t = pv_dot(p_all[:, k * PAGE_SIZE:(k + 1) * PAGE_SIZE], values[k])
            pv = t if pv is None else pv + t
        acc_sc[...] = alpha * acc_sc[...] + pv
        m_sc[...] = m_new

    @pl.when(c == n_chunks - 1)
    def _():
        if mode == "sb":
            o_ref[...] = acc_sc[...]
        else:
            o_ref[...] = acc_sc[...] / l_sc[...]


def paged_decode(mode, q, k_new, v_new, pool, page_table, layer, lf_new=None, lf_pool_t=None):
    bd, n_heads, dq = q.shape
    dv = v_new.shape[-1]
    n_pages = page_table.shape[1]
    np_ = min(DECODE_PAGES, n_pages)
    assert n_pages % np_ == 0
    scale = (MLA_NOPE + MLA_ROPE) ** -0.5 if mode == "mla" else HEAD_DIM ** -0.5
    if mode == "mla":
        pool = jnp.swapaxes(pool, 2, 3)
    else:
        pool = pool.reshape(pool.shape[0], pool.shape[1], PAGE_SIZE * 4, HEAD_DIM)
    page_block = (None, None) + pool.shape[2:]
    zeros = (0,) * (pool.ndim - 2)

    def page_spec(k, block, tail):
        def imap(b, c, pt):
            return (layer, pt[b, n_pages - 1 - (c * np_ + k)]) + tail
        return pl.BlockSpec(block, imap)

    def head_spec(d):
        return pl.BlockSpec((None, n_heads, d), lambda b, c, pt: (b, 0, 0))

    in_specs = [head_spec(dq), head_spec(dq), head_spec(dv)]
    args = [q, k_new, v_new]
    if mode == "fox":
        in_specs.append(head_spec(1))
        args.append(lf_new)
    in_specs += [page_spec(k, page_block, zeros) for k in range(np_)]
    args += [pool] * np_
    if mode == "fox":
        in_specs += [page_spec(k, (None, None, n_heads, PAGE_SIZE), (0, 0)) for k in range(np_)]
        args += [lf_pool_t] * np_
    return pl.pallas_call(
        functools.partial(_decode_kernel, mode=mode, n_pages_step=np_, scale=scale),
        grid_spec=pltpu.PrefetchScalarGridSpec(
            num_scalar_prefetch=1,
            grid=(bd, n_pages // np_),
            in_specs=in_specs,
            out_specs=pl.BlockSpec((None, n_heads, dv), lambda b, c, pt: (b, 0, 0)),
            scratch_shapes=[pltpu.VMEM((n_heads, 1), jnp.float32), pltpu.VMEM((n_heads, 1), jnp.float32),
                            pltpu.VMEM((n_heads, dv), jnp.float32), pltpu.VMEM((n_heads, 1), jnp.float32)],
        ),
        out_shape=jax.ShapeDtypeStruct((bd, n_heads, dv), jnp.float32),
        compiler_params=pltpu.CompilerParams(
            dimension_semantics=("arbitrary", "arbitrary"), vmem_limit_bytes=V7X_VMEM_LIMIT_BYTES),
        name="paged_decode_" + mode,
    )(page_table, *args)


def rms_norm(x, g):
    xf = x.astype(jnp.float32)
    y = xf * lax.rsqrt(jnp.mean(xf * xf, axis=-1, keepdims=True) + RMS_EPS)
    return (y * g.astype(jnp.float32)).astype(x.dtype)


def rope_rows(x, pos, rot_dim):
    half = rot_dim // 2
    inv_freq = jnp.power(ROPE_THETA, -jnp.arange(half, dtype=jnp.float32) * 2.0 / rot_dim)
    ang = pos.astype(jnp.float32)[:, None] * inv_freq[None, :]
    cos = jnp.cos(ang)[:, None, :]
    sin = jnp.sin(ang)[:, None, :]
    x1, x2 = x[..., :half], x[..., half:rot_dim]
    return jnp.concatenate([x1 * cos - x2 * sin, x2 * cos + x1 * sin, x[..., rot_dim:]], axis=-1)


def kernel(x_prompt, x_sample, cache_a_latent, cache_b_kv_w128, cache_b_kv_w512, cache_b_kv_w2048,
           cache_c_kv, cache_c_logf, cache_d_kv, page_table,
           ffn_norm_g, ffn_w_gu, ffn_w_down, mix_norm_g,
           a_w_down, a_q_norm_g, a_kv_norm_g, a_w_uq, a_w_uk, a_w_uv, a_w_o,
           b_w_in, b_w_o, c_w_in, c_b_f, c_w_o, d_w_in, d_w_o, final_norm_g):
    depth = ffn_norm_g.shape[0]
    bp, S, _ = x_prompt.shape
    bs, T, _ = x_sample.shape
    assert T == 1
    past = page_table.shape[1] * PAGE_SIZE
    b_caches = (cache_b_kv_w128, cache_b_kv_w512, cache_b_kv_w2048)
    xs2 = [x_prompt.reshape(bp * S, D_MODEL), x_sample.reshape(bs * T, D_MODEL)]
    pos = [jnp.tile(jnp.arange(S, dtype=jnp.int32), bp), jnp.full((bs * T,), past, jnp.int32)]
    lead = [(bp, S), (bs, T)]

    a_out, b_out, ckv_out, clf_out, dkv_out = [[], []], [[[], []] for _ in range(DIL_GROUPS)], [[], []], [[], []], [[], []]

    for i in range(depth):
        m, j = i % N_MIXERS, i // N_MIXERS
        xs2 = [ffn_half(x, ffn_norm_g[i, 0], ffn_w_gu[i, 0], ffn_w_down[i, 0]) for x in xs2]
        mixed = []
        for grp, x in enumerate(xs2):
            nb, nt = lead[grp]
            rows = nb * nt
            if m == 0:
                d = linear(x, a_w_down[j], norm_g=mix_norm_g[i])
                c_q = rms_norm(d[:, :MLA_Q_LORA], a_q_norm_g[j])
                c_kv = rms_norm(d[:, MLA_Q_LORA:MLA_Q_LORA + MLA_KV_LORA], a_kv_norm_g[j])
                k_rope = rope_rows(d[:, None, MLA_Q_LORA + MLA_KV_LORA:], pos[grp], MLA_ROPE)[:, 0]
                latent = jnp.concatenate([c_kv, k_rope], axis=-1)
                q = linear(c_q, a_w_uq[j]).reshape(rows, MLA_HEADS, MLA_NOPE + MLA_ROPE)
                q_rope = rope_rows(q[..., MLA_NOPE:], pos[grp], MLA_ROPE)
                q_lat = headwise_matmul(q[..., :MLA_NOPE].reshape(rows, MLA_HEADS * MLA_NOPE),
                                        jnp.transpose(a_w_uk[j], (1, 2, 0)))
                if grp == 0:
                    o_lat = mla_attention_prompt(q_lat.reshape(nb, nt, -1), q_rope.reshape(nb, nt, -1),
                                                 latent.reshape(nb, nt, -1)).reshape(rows, -1)
                else:
                    q_all = jnp.concatenate([q_lat.reshape(rows, MLA_HEADS, MLA_KV_LORA), q_rope], axis=-1)
                    k_new = jnp.broadcast_to(latent[:, None, :], (rows, MLA_HEADS, MLA_KV_LORA + MLA_ROPE))
                    o_lat = paged_decode("mla", q_all, k_new, k_new[..., :MLA_KV_LORA], cache_a_latent,
                                         page_table, j).reshape(rows, -1)
                o = headwise_matmul(o_lat, jnp.transpose(a_w_uv[j], (1, 0, 2)))
                mixed.append(linear(o, a_w_o[j], residual=x))
                a_out[grp].append(latent.reshape(nb, nt, -1))
            elif m == 1:
                n_dil = DIL_GROUPS * DIL_HEADS * HEAD_DIM
                tables = rope_tables(pos[grp][:min(rows, S)], ROT_DIM)
                p = linear_rope(x, b_w_in[j], mix_norm_g[i], tables, 2 * n_dil, n_dil)
                kv = p[:, n_dil:].reshape(nb, nt, 2, DIL_GROUPS, DIL_HEADS, HEAD_DIM)
                if grp == 0:
                    o = dilated_attention_prompt(p.reshape(nb, nt, 3 * n_dil)).reshape(rows, -1)
                else:
                    o = dilated_attention_sample(p[:, :n_dil].reshape(rows, DIL_GROUPS, DIL_HEADS, HEAD_DIM),
                                                 kv[:, 0], b_caches, j).reshape(rows, -1)
                mixed.append(linear(o, b_w_o[j], residual=x))
                for g, (win, _) in enumerate(DIL_PATTERNS):
                    b_out[g][grp].append(kv[:, nt - min(win, nt):, :, g])
            elif m == 2:
                nq, nkv = FOX_HEADS * HEAD_DIM, FOX_KV_HEADS * HEAD_DIM
                rep = FOX_HEADS // FOX_KV_HEADS
                p = linear(x, c_w_in[j], norm_g=mix_norm_g[i])
                kv = p[:, nq:nq + 2 * nkv].reshape(nb, nt, 2, FOX_KV_HEADS, HEAD_DIM)
                logf = jax.nn.log_sigmoid(p[:, nq + 2 * nkv:] + c_b_f[j]).reshape(nb, nt, FOX_HEADS)
                if grp == 0:
                    o = fox_attention_prompt(p.reshape(nb, nt, -1), jnp.cumsum(logf, axis=1)).reshape(rows, -1)
                else:
                    o = paged_decode("fox", p[:, :nq].reshape(rows, FOX_HEADS, HEAD_DIM),
                                     jnp.repeat(kv[:, 0, 0], rep, axis=1), jnp.repeat(kv[:, 0, 1], rep, axis=1),
                                     cache_c_kv, page_table, j, logf.reshape(rows, FOX_HEADS, 1),
                                     jnp.swapaxes(cache_c_logf, 2, 3)).reshape(rows, -1)
                mixed.append(linear(o, c_w_o[j], residual=x))
                ckv_out[grp].append(kv)
                clf_out[grp].append(logf)
            else:
                nq, nkv = SB_HEADS * HEAD_DIM, SB_KV_HEADS * HEAD_DIM
                rep = SB_HEADS // SB_KV_HEADS
                p = linear(x, d_w_in[j], norm_g=mix_norm_g[i])
                kv = p[:, nq:nq + 2 * nkv].reshape(nb, nt, 2, SB_KV_HEADS, HEAD_DIM)
                if grp == 0:
                    o = sb_attention_prompt(p.reshape(nb, nt, -1)).reshape(rows, -1)
                else:
                    o = paged_decode("sb", p[:, :nq].reshape(rows, SB_HEADS, HEAD_DIM),
                                     jnp.repeat(kv[:, 0, 0], rep, axis=1), jnp.repeat(kv[:, 0, 1], rep, axis=1),
                                     cache_d_kv, page_table, j).reshape(rows, -1)
                mixed.append(linear(o, d_w_o[j], residual=x))
                dkv_out[grp].append(kv)
        xs2 = [ffn_half(x, ffn_norm_g[i, 1], ffn_w_gu[i, 1], ffn_w_down[i, 1]) for x in mixed]

    y_prompt = rms_norm(xs2[0], final_norm_g).reshape(bp, S, D_MODEL)
    y_sample = rms_norm(xs2[1], final_norm_g).reshape(bs, T, D_MODEL)
    return (y_prompt, y_sample, jnp.stack(a_out[0]), jnp.stack(a_out[1]),
            jnp.stack(b_out[0][0]), jnp.stack(b_out[0][1]), jnp.stack(b_out[1][0]), jnp.stack(b_out[1][1]),
            jnp.stack(b_out[2][0]), jnp.stack(b_out[2][1]),
            jnp.stack(ckv_out[0]), jnp.stack(ckv_out[1]), jnp.stack(clf_out[0]), jnp.stack(clf_out[1]),
            jnp.stack(dkv_out[0]), jnp.stack(dkv_out[1]))
```
